```python
import math
import jax, jax.numpy as jnp
from jax import lax
import numpy as np

D_MODEL = 1024
BATCH = 4
SEQ = 4096
DEPTH = 1
DEC_BATCH = 32
DEC_SEQ = 1
PAST_LEN = 8192
PAGE_SIZE = 128

DIFF_HEADS = 8
DIFF_DH = 64
DIFF_DV = 2 * DIFF_DH
DIFF_WIDTH = DIFF_HEADS * DIFF_DV
Q_BLOCK = 128
MLSTM_HEADS = 4
MLSTM_DK = 128
MLSTM_DV = 256
MLSTM_WIDTH = MLSTM_HEADS * MLSTM_DV
MLSTM_CHUNK = 64
PEER_HEADS = 8
PEER_NKEYS = 128
PEER_EXPERTS = PEER_NKEYS * PEER_NKEYS
PEER_DKEY = 256
PEER_HALF = PEER_DKEY // 2
PEER_TOPK = 16
PEER_BLOCK = 128

RMS_EPS = 1e-6
NEG_INF = -1e30

IN_SIZES = (DIFF_HEADS * 2 * DIFF_DH, DIFF_HEADS * 2 * DIFF_DH, DIFF_WIDTH,
            MLSTM_HEADS * MLSTM_DK, MLSTM_HEADS * MLSTM_DK, MLSTM_WIDTH,
            MLSTM_WIDTH, MLSTM_HEADS, MLSTM_HEADS,
            2 * D_MODEL)
IN_SPLITS = tuple(int(s) for s in np.cumsum(IN_SIZES)[:-1])
IN_COLS = int(sum(IN_SIZES))

kernel_name = "diffattn_mlstm_peer_hybrid_step"


def rmsnorm(x, g):
    xf = x.astype(jnp.float32)
    y = xf * lax.rsqrt(jnp.mean(xf * xf, axis=-1, keepdims=True) + RMS_EPS)
    return (y * g.astype(jnp.float32)).astype(x.dtype)


def alibi_slopes():
    return jnp.exp2(-8.0 * jnp.arange(1, DIFF_HEADS + 1, dtype=jnp.float32) / DIFF_HEADS)


def diff_lambda(lq1, lk1, lq2, lk2, lam_init):
    f = jnp.float32
    return (jnp.exp(jnp.sum(lq1.astype(f) * lk1.astype(f)))
            - jnp.exp(jnp.sum(lq2.astype(f) * lk2.astype(f))) + lam_init)


def diff_attend(q, k, v, q_pos, k_pos, lam):
    s = jnp.einsum('bqhcd,bkhcd->bhcqk', q, k).astype(jnp.float32) * (DIFF_DH ** -0.5)
    dist = (q_pos[:, None] - k_pos[None, :]).astype(jnp.float32)
    bias = -alibi_slopes()[:, None, None] * dist
    causal = q_pos[:, None] >= k_pos[None, :]
    s = jnp.where(causal, s + bias[None, :, None], NEG_INF)
    p = jax.nn.softmax(s, axis=-1)
    a = p[:, :, 0] - lam * p[:, :, 1]
    return jnp.einsum('bhqk,bkhd->bqhd', a.astype(v.dtype), v)


def mlstm_chunk(carry, inp):
    C0, n0, m0 = carry
    q, k, v, ig, lf = inp
    L = q.shape[2]
    F = jnp.cumsum(lf, axis=-1)
    causal = jnp.tril(jnp.ones((L, L), dtype=bool))
    logw = jnp.where(causal, F[..., :, None] - F[..., None, :] + ig[..., None, :], NEG_INF)
    inter = m0[..., None] + F
    m = jnp.maximum(inter, jnp.max(logw, axis=-1))
    w = jnp.where(causal, jnp.exp(logw - m[..., None]), 0.0)
    a_inter = jnp.exp(inter - m)
    qk = jnp.einsum('bhtd,bhsd->bhts', q, k) * w
    num = (a_inter[..., None] * jnp.einsum('bhtd,bhde->bhte', q, C0)
           + jnp.einsum('bhts,bhse->bhte', qk, v))
    den = a_inter * jnp.einsum('bhtd,bhd->bht', q, n0) + jnp.sum(qk, axis=-1)
    h = num / jnp.maximum(jnp.abs(den), jnp.exp(-m))[..., None]
    m_end = m[..., -1]
    w_end = jnp.exp(F[..., -1:] - F + ig - m_end[..., None])
    decay = jnp.exp(inter[..., -1] - m_end)
    C = decay[..., None, None] * C0 + jnp.einsum('bhs,bhsd,bhse->bhde', w_end, k, v)
    n = decay[..., None] * n0 + jnp.einsum('bhs,bhsd->bhd', w_end, k)
    return (C, n, m_end), h


def mlstm_scan(q, k, v, ig, lf, C0, n0, m0):
    B, T = q.shape[:2]
    L = MLSTM_CHUNK if T % MLSTM_CHUNK == 0 else T
    nc = T // L

    def chunks(a):
        a = a.reshape((B, nc, L) + a.shape[2:])
        return jnp.moveaxis(jnp.moveaxis(a, 1, 0), 3, 2)

    (C, n, m), h = lax.scan(mlstm_chunk, (C0, n0, m0),
                            (chunks(q), chunks(k), chunks(v), chunks(ig), chunks(lf)))
    h = jnp.moveaxis(jnp.moveaxis(h, 0, 1), 2, 3).reshape(B, T, MLSTM_HEADS, MLSTM_DV)
    return h, C, n, m


def mixer_inputs(x, norm_g, w_in, q_norm_g, k_norm_g, b_i, b_f):
    B, T, _ = x.shape
    z = rmsnorm(x, norm_g) @ w_in
    dq, dk, dv, mq, mk, mv, mo, mi, mf, gates = jnp.split(z, IN_SPLITS, axis=-1)
    dq = rmsnorm(dq.reshape(B, T, DIFF_HEADS, 2, DIFF_DH), q_norm_g)
    dk = rmsnorm(dk.reshape(B, T, DIFF_HEADS, 2, DIFF_DH), k_norm_g)
    dv = dv.reshape(B, T, DIFF_HEADS, DIFF_DV)
    mq = mq.reshape(B, T, MLSTM_HEADS, MLSTM_DK).astype(jnp.float32)
    mk = mk.reshape(B, T, MLSTM_HEADS, MLSTM_DK).astype(jnp.float32) * (MLSTM_DK ** -0.5)
    mv = mv.reshape(B, T, MLSTM_HEADS, MLSTM_DV).astype(jnp.float32)
    ig = (mi + b_i).astype(jnp.float32)
    lf = jax.nn.log_sigmoid((mf + b_f).astype(jnp.float32))
    return dq, dk, dv, mq, mk, mv, jax.nn.sigmoid(mo), ig, lf, gates


def merge_branches(x, o_diff, h_mlstm, o_gate, gates, subln_g, mnorm_g, w_a, w_b, w_out, lam_init):
    B, T, _ = x.shape
    a = (rmsnorm(o_diff, subln_g) * (1.0 - lam_init)).reshape(B, T, DIFF_WIDTH)
    hb = rmsnorm(h_mlstm.astype(x.dtype), mnorm_g).reshape(B, T, MLSTM_WIDTH) * o_gate
    g_a, g_b = jnp.split(jax.nn.sigmoid(gates), 2, axis=-1)
    y = (g_a * (a @ w_a) + g_b * (hb @ w_b)) @ w_out
    return x + y


def peer(xn, wq, subkeys, u, v):
    n = xn.shape[0]
    blk = min(PEER_BLOCK, n)
    n_pad = -(-n // blk) * blk
    xp = jnp.pad(xn, ((0, n_pad - n), (0, 0))).reshape(n_pad // blk, blk, D_MODEL)

    def one(xb):
        q = (xb @ wq).reshape(blk, PEER_HEADS, 2, PEER_HALF)
        s = jnp.einsum('nhcd,ckd->nhck', q, subkeys).astype(jnp.float32)
        s1, i1 = lax.top_k(s[:, :, 0], PEER_TOPK)
        s2, i2 = lax.top_k(s[:, :, 1], PEER_TOPK)
        cand = (s1[..., :, None] + s2[..., None, :]).reshape(blk, PEER_HEADS, PEER_TOPK * PEER_TOPK)
        cidx = (i1[..., :, None] * PEER_NKEYS + i2[..., None, :]).reshape(blk, PEER_HEADS, PEER_TOPK * PEER_TOPK)
        top_s, pos = lax.top_k(cand, PEER_TOPK)
        eidx = jnp.take_along_axis(cidx, pos, axis=-1)
        g = jax.nn.softmax(top_s, axis=-1)
        ue = u[eidx]
        ve = v[eidx]
        act = jax.nn.gelu(jnp.einsum('nd,nhkd->nhk', xb, ue).astype(jnp.float32))
        return jnp.einsum('nhk,nhkd->nd', (g * act).astype(xb.dtype), ve)

    return lax.map(one, xp).reshape(n_pad, D_MODEL)[:n]


def channel_mix(x, norm_g, wq, subkeys, u, v):
    out = peer(rmsnorm(x, norm_g).reshape(-1, D_MODEL), wq, subkeys, u, v)
    return x + out.reshape(x.shape)


def setup_inputs(seed: int = 0) -> dict:
    key = jax.random.key(seed)
    ks = jax.random.split(key, 32)
    f32 = jnp.float32
    n_pages = PAST_LEN // PAGE_SIZE
    n_pool = (DEC_BATCH * n_pages * 5) // 4

    def nrm(k, shape, scale):
        return jax.random.normal(k, shape, f32) * scale

    def gain(k, shape):
        return 1.0 + 0.02 * jax.random.normal(k, shape, f32)

    page_table = jax.random.permutation(ks[7], n_pool)[:DEC_BATCH * n_pages]
    page_table = page_table.reshape(DEC_BATCH, n_pages).astype(jnp.int32)
    return {
        "x_prompt": nrm(ks[0], (BATCH, SEQ, D_MODEL), 1.0),
        "x_sample": nrm(ks[1], (DEC_BATCH, DEC_SEQ, D_MODEL), 1.0),
        "cache_k": nrm(ks[2], (DEPTH, n_pool, PAGE_SIZE, DIFF_HEADS, 2 * DIFF_DH), 1.0),
        "cache_v": nrm(ks[3], (DEPTH, n_pool, PAGE_SIZE, DIFF_HEADS, DIFF_DV), 1.0),
        "state_C": nrm(ks[4], (DEPTH, DEC_BATCH, MLSTM_HEADS, MLSTM_DK, MLSTM_DV), 0.05),
        "state_n": nrm(ks[5], (DEPTH, DEC_BATCH, MLSTM_HEADS, MLSTM_DK), 0.5),
        "state_m": jax.random.uniform(ks[6], (DEPTH, DEC_BATCH, MLSTM_HEADS), f32, -2.0, 2.0),
        "page_table": page_table,
        "norm1_g": gain(ks[8], (DEPTH, D_MODEL)),
        "w_in": nrm(ks[9], (DEPTH, D_MODEL, IN_COLS), D_MODEL ** -0.5),
        "q_norm_g": gain(ks[10], (DEPTH, DIFF_DH)),
        "k_norm_g": gain(ks[11], (DEPTH, DIFF_DH)),
        "lam_q1": nrm(ks[12], (DEPTH, DIFF_DH), 0.1),
        "lam_k1": nrm(ks[13], (DEPTH, DIFF_DH), 0.1),
        "lam_q2": nrm(ks[14], (DEPTH, DIFF_DH), 0.1),
        "lam_k2": nrm(ks[15], (DEPTH, DIFF_DH), 0.1),
        "diff_subln_g": gain(ks[16], (DEPTH, DIFF_HEADS, DIFF_DV)),
        "b_i": nrm(ks[17], (DEPTH, MLSTM_HEADS), 0.1),
        "b_f": 3.0 + jax.random.uniform(ks[18], (DEPTH, MLSTM_HEADS), f32, 0.0, 3.0),
        "mlstm_norm_g": gain(ks[19], (DEPTH, MLSTM_HEADS, MLSTM_DV)),
        "w_branch_a": nrm(ks[20], (DEPTH, DIFF_WIDTH, D_MODEL), DIFF_WIDTH ** -0.5),
        "w_branch_b": nrm(ks[21], (DEPTH, MLSTM_WIDTH, D_MODEL), MLSTM_WIDTH ** -0.5),
        "w_out": nrm(ks[22], (DEPTH, D_MODEL, D_MODEL), D_MODEL ** -0.5),
        "norm2_g": gain(ks[23], (DEPTH, D_MODEL)),
        "peer_wq": nrm(ks[24], (DEPTH, D_MODEL, PEER_HEADS * PEER_DKEY), D_MODEL ** -0.5),
        "peer_subkeys": nrm(ks[25], (DEPTH, 2, PEER_NKEYS, PEER_HALF), PEER_HALF ** -0.5),
        "peer_u": nrm(ks[26], (DEPTH, PEER_EXPERTS, D_MODEL), D_MODEL ** -0.5),
        "peer_v": nrm(ks[27], (DEPTH, PEER_EXPERTS, D_MODEL), (PEER_HEADS * PEER_TOPK) ** -0.5),
    }


def reference(x_prompt, x_sample, cache_k, cache_v, state_C, state_n, state_m, page_table,
              norm1_g, w_in, q_norm_g, k_norm_g, lam_q1, lam_k1, lam_q2, lam_k2, diff_subln_g,
              b_i, b_f, mlstm_norm_g, w_branch_a, w_branch_b, w_out, norm2_g,
              peer_wq, peer_subkeys, peer_u, peer_v):
    n_pages = PAST_LEN // PAGE_SIZE
    xp, xs = x_prompt, x_sample
    Bp, T, _ = xp.shape
    Bs, Ts, _ = xs.shape
    kp_l, vp_l, Cp_l, np_l, mp_l = [], [], [], [], []
    ks_l, vs_l, Cs_l, ns_l, ms_l = [], [], [], [], []
    for l in range(DEPTH):
        lam_init = 0.8 - 0.6 * math.exp(-0.3 * l)
        lam = diff_lambda(lam_q1[l], lam_k1[l], lam_q2[l], lam_k2[l], lam_init)

        dq, dk, dv, mq, mk, mv, mo, ig, lf, gates = mixer_inputs(
            xp, norm1_g[l], w_in[l], q_norm_g[l], k_norm_g[l], b_i[l], b_f[l])
        k_pos = jnp.arange(T, dtype=jnp.int32)

        def q_block(i, dq=dq, dk=dk, dv=dv, k_pos=k_pos, lam=lam):
            start = i * Q_BLOCK
            qb = lax.dynamic_slice_in_dim(dq, start, Q_BLOCK, axis=1)
            q_pos = start + jnp.arange(Q_BLOCK, dtype=jnp.int32)
            return diff_attend(qb, dk, dv, q_pos, k_pos, lam)

        o_diff = lax.map(q_block, jnp.arange(T // Q_BLOCK, dtype=jnp.int32))
        o_diff = jnp.moveaxis(o_diff, 0, 1).reshape(Bp, T, DIFF_HEADS, DIFF_DV)
        C0 = jnp.zeros((Bp, MLSTM_HEADS, MLSTM_DK, MLSTM_DV), jnp.float32)
        n0 = jnp.zeros((Bp, MLSTM_HEADS, MLSTM_DK), jnp.float32)
        m0 = jnp.zeros((Bp, MLSTM_HEADS), jnp.float32)
        h, Cp, npr, mp = mlstm_scan(mq, mk, mv, ig, lf, C0, n0, m0)
        xp = merge_branches(xp, o_diff, h, mo, gates, diff_subln_g[l], mlstm_norm_g[l],
                            w_branch_a[l], w_branch_b[l], w_out[l], lam_init)
        xp = channel_mix(xp, norm2_g[l], peer_wq[l], peer_subkeys[l], peer_u[l], peer_v[l])
        kp_l.append(dk.reshape(Bp, T, DIFF_HEADS, 2 * DIFF_DH))
        vp_l.append(dv)
        Cp_l.append(Cp)
        np_l.append(npr)
        mp_l.append(mp)

        dq, dk, dv, mq, mk, mv, mo, ig, lf, gates = mixer_inputs(
            xs, norm1_g[l], w_in[l], q_norm_g[l], k_norm_g[l], b_i[l], b_f[l])
        k_past = cache_k[l][page_table].reshape(Bs, n_pages * PAGE_SIZE, DIFF_HEADS, 2, DIFF_DH)
        v_past = cache_v[l][page_table].reshape(Bs, n_pages * PAGE_SIZE, DIFF_HEADS, DIFF_DV)
        k_all = jnp.concatenate([k_past, dk.astype(k_past.dtype)], axis=1)
        v_all = jnp.concatenate([v_past, dv.astype(v_past.dtype)], axis=1)
        q_pos = PAST_LEN + jnp.arange(Ts, dtype=jnp.int32)
        k_pos = jnp.arange(PAST_LEN + Ts, dtype=jnp.int32)
        o_diff = diff_attend(dq, k_all, v_all, q_pos, k_pos, lam)
        h, Cs, ns, ms = mlstm_scan(mq, mk, mv, ig, lf, state_C[l].astype(jnp.float32),
                                   state_n[l].astype(jnp.float32), state_m[l].astype(jnp.float32))
        xs = merge_branches(xs, o_diff, h, mo, gates, diff_subln_g[l], mlstm_norm_g[l],
                            w_branch_a[l], w_branch_b[l], w_out[l], lam_init)
        xs = channel_mix(xs, norm2_g[l], peer_wq[l], peer_subkeys[l], peer_u[l], peer_v[l])
        ks_l.append(dk.reshape(Bs, Ts, DIFF_HEADS, 2 * DIFF_DH))
        vs_l.append(dv)
        Cs_l.append(Cs)
        ns_l.append(ns)
        ms_l.append(ms)

    k_prompt = jnp.stack(kp_l)
    v_prompt = jnp.stack(vp_l)
    C_prompt = jnp.stack(Cp_l)
    n_prompt = jnp.stack(np_l)
    m_prompt = jnp.stack(mp_l)
    k_sample = jnp.stack(ks_l)
    v_sample = jnp.stack(vs_l)
    C_sample = jnp.stack(Cs_l)
    n_sample = jnp.stack(ns_l)
    m_sample = jnp.stack(ms_l)
    return (xp, xs, k_prompt, v_prompt, C_prompt, n_prompt, m_prompt,
            k_sample, v_sample, C_sample, n_sample, m_sample)
```

```python
import functools
import math

import jax
import jax.numpy as jnp
from jax import lax
from jax.experimental import pallas as pl
from jax.experimental.pallas import tpu as pltpu

F32 = jnp.float32
BF16 = jnp.bfloat16
HIGHEST = lax.Precision.HIGHEST

RMS_EPS = 1e-6
NEG_INF = -1e30

DIFF_HEADS = 8
DIFF_DH = 64
DIFF_DV = 2 * DIFF_DH
MLSTM_HEADS = 4
MLSTM_DK = 128
MLSTM_DV = 256
PEER_HEADS = 8
PEER_NKEYS = 128
PEER_HALF = 128
PEER_TOPK = 16
PAGE_SIZE = 128

LANES = 128
COL_BLOCK = 1024
VMEM_LIMIT = 48 * 1024 * 1024


def _cparams(sem):
    return pltpu.CompilerParams(dimension_semantics=sem, vmem_limit_bytes=VMEM_LIMIT)


def _log_sigmoid(x):
    return jnp.minimum(x, 0.0) - jnp.log1p(jnp.exp(-jnp.abs(x)))


def _sigmoid(x):
    return 1.0 / (1.0 + jnp.exp(-x))


def _in_proj_kernel(x_ref, g1_ref, w_ref, wg_ref, qg_ref, kg_ref, gsum_ref, brow_ref, cs_ref,
                    qn_ref, kf_ref, kb_ref, vf_ref, vb_ref, mqk_ref, mv_ref, og_ref, gates_ref,
                    gcol_ref, grow_ref, xn_scr):
    j = pl.program_id(1)

    @pl.when(j == 0)
    def _():
        x = x_ref[...]
        xn = x * lax.rsqrt(jnp.mean(x * x, axis=-1, keepdims=True) + RMS_EPS) * g1_ref[...]
        xn_scr[...] = xn.astype(BF16)
        zc = jnp.dot(xn, wg_ref[...], precision=HIGHEST, preferred_element_type=F32) + brow_ref[...]
        lane = lax.broadcasted_iota(jnp.int32, zc.shape, 1)
        gc = jnp.where(lane < MLSTM_HEADS, zc, _log_sigmoid(zc))
        gcol_ref[...] = gc
        grow_ref[...] = jnp.transpose(gc)[:8, :]

    z = jnp.dot(xn_scr[...], w_ref[...], preferred_element_type=F32)

    def head_norm(gain):
        ss = jnp.dot((z * z).astype(BF16), gsum_ref[...], preferred_element_type=F32)
        return z * lax.rsqrt(ss * (1.0 / DIFF_DH) + RMS_EPS) * gain

    @pl.when(j == 0)
    def _():
        qn_ref[...] = (head_norm(qg_ref[...]) * (DIFF_DH ** -0.5)).astype(qn_ref.dtype)

    @pl.when(j == 1)
    def _():
        kn = head_norm(kg_ref[...])
        kf_ref[...] = kn
        kb_ref[...] = kn.astype(kb_ref.dtype)

    @pl.when(j == 2)
    def _():
        vf_ref[...] = z
        vb_ref[...] = z.astype(vb_ref.dtype)

    @pl.when(j == 3)
    def _():
        mqk_ref[...] = (z * cs_ref[...]).astype(mqk_ref.dtype)

    @pl.when(j == 4)
    def _():
        mv_ref[...] = z.astype(mv_ref.dtype)

    @pl.when(j == 5)
    def _():
        og_ref[...] = _sigmoid(z).astype(og_ref.dtype)

    @pl.when(j >= 6)
    def _():
        gates_ref[...] = _sigmoid(z).astype(gates_ref.dtype)


def _in_proj(x2d, norm1_g, w_main, w_gate, qg_t, kg_t, gsum, brow, colscale, *, tm, act_dtype):
    n, d = x2d.shape
    nblk = w_main.shape[1] // COL_BLOCK
    assert nblk == 8 and n % tm == 0
    row_blk = lambda i, j: (i, 0)
    const = lambda i, j: (0, 0)
    wide = pl.BlockSpec((tm, COL_BLOCK), row_blk)
    out_shape = (
        jax.ShapeDtypeStruct((n, COL_BLOCK), act_dtype),
        jax.ShapeDtypeStruct((n, COL_BLOCK), F32),
        jax.ShapeDtypeStruct((n, COL_BLOCK), act_dtype),
        jax.ShapeDtypeStruct((n, COL_BLOCK), F32),
        jax.ShapeDtypeStruct((n, COL_BLOCK), act_dtype),
        jax.ShapeDtypeStruct((n, COL_BLOCK), act_dtype),
        jax.ShapeDtypeStruct((n, COL_BLOCK), act_dtype),
        jax.ShapeDtypeStruct((n, COL_BLOCK), act_dtype),
        jax.ShapeDtypeStruct((n, 2 * COL_BLOCK), act_dtype),
        jax.ShapeDtypeStruct((n, LANES), F32),
        jax.ShapeDtypeStruct((8, n), F32),
    )
    out_specs = (wide,) * 8 + (
        pl.BlockSpec((tm, COL_BLOCK), lambda i, j: (i, jnp.maximum(j - 6, 0))),
        pl.BlockSpec((tm, LANES), row_blk),
        pl.BlockSpec((8, tm), lambda i, j: (0, i)),
    )
    return pl.pallas_call(
        _in_proj_kernel,
        grid=(n // tm, nblk),
        in_specs=[
            pl.BlockSpec((tm, d), row_blk),
            pl.BlockSpec((1, d), const),
            pl.BlockSpec((d, COL_BLOCK), lambda i, j: (0, j)),
            pl.BlockSpec((d, LANES), const),
            pl.BlockSpec((1, COL_BLOCK), const),
            pl.BlockSpec((1, COL_BLOCK), const),
            pl.BlockSpec((COL_BLOCK, COL_BLOCK), const),
            pl.BlockSpec((1, LANES), const),
            pl.BlockSpec((1, COL_BLOCK), const),
        ],
        out_specs=out_specs,
        out_shape=out_shape,
        scratch_shapes=[pltpu.VMEM((tm, d), BF16)],
        compiler_params=_cparams(("parallel", "arbitrary")),
        name="in_proj",
    )(x2d, norm1_g, w_main, w_gate, qg_t, kg_t, gsum, brow, colscale)


def _prep_in_proj(w_in, q_norm_g, k_norm_g, b_i, b_f):
    d = w_in.shape[0]
    n_main = 3 * COL_BLOCK + 2 * MLSTM_HEADS * MLSTM_DK + 2 * MLSTM_HEADS * MLSTM_DV
    n_gate = 2 * MLSTM_HEADS
    assert w_in.shape[1] == n_main + n_gate + 2 * d and n_main == 6 * COL_BLOCK
    w_main = jnp.concatenate([w_in[:, :n_main], w_in[:, n_main + n_gate:]], axis=1).astype(BF16)
    w_gate = jnp.pad(w_in[:, n_main:n_main + n_gate], ((0, 0), (0, LANES - n_gate)))
    qg_t = jnp.tile(q_norm_g.astype(F32), COL_BLOCK // DIFF_DH).reshape(1, COL_BLOCK)
    kg_t = jnp.tile(k_norm_g.astype(F32), COL_BLOCK // DIFF_DH).reshape(1, COL_BLOCK)
    grp = jnp.arange(COL_BLOCK, dtype=jnp.int32) // DIFF_DH
    gsum = (grp[:, None] == grp[None, :]).astype(BF16)
    brow = jnp.pad(jnp.concatenate([b_i, b_f]).astype(F32), (0, LANES - n_gate)).reshape(1, LANES)
    half = MLSTM_HEADS * MLSTM_DK
    colscale = jnp.concatenate([jnp.ones((half,), F32), jnp.full((half,), MLSTM_DK ** -0.5, F32)]).reshape(1, COL_BLOCK)
    return w_main, w_gate, qg_t, kg_t, gsum, brow, colscale


def _diff_lambda(lq1, lk1, lq2, lk2, lam_init):
    e1 = jnp.exp(jnp.sum(lq1 * lk1, axis=-1, keepdims=True))
    e2 = jnp.exp(jnp.sum(lq2 * lk2, axis=-1, keepdims=True))
    return e1 - e2 + lam_init


def _attn_kernel(slopes_ref, lam_ref, q_ref, k_ref, v_ref, sg_ref, o_ref, *, tq, lam_init):
    h = pl.program_id(1)
    qi = pl.program_id(2)
    slope = slopes_ref[h]
    lam = _diff_lambda(lam_ref[0:1, :], lam_ref[1:2, :], lam_ref[2:3, :], lam_ref[3:4, :], lam_init)
    q = q_ref[...]
    q1, q2 = q[:, :DIFF_DH], q[:, DIFF_DH:]
    nt = (((1,), (1,)), ((), ()))
    kcol = lax.broadcasted_iota(jnp.int32, (1, tq), 1)

    def tile(j, carry, diag):
        m1, l1, a1, m2, l2, a2 = carry
        start = pl.multiple_of(j * tq, tq)
        kt = k_ref[pl.ds(start, tq), :]
        vt = v_ref[pl.ds(start, tq), :]
        bias = slope * ((j - qi) * tq + kcol).astype(F32)
        if diag:
            row = lax.broadcasted_iota(jnp.int32, (tq, tq), 0)
            col = lax.broadcasted_iota(jnp.int32, (tq, tq), 1)
            keep = col <= row

        def comp(qc, kc, m, l, a):
            s = lax.dot_general(qc, kc, nt, preferred_element_type=F32) + bias
            if diag:
                s = jnp.where(keep, s, NEG_INF)
            m_new = jnp.maximum(m, jnp.max(s, axis=-1, keepdims=True))
            alpha = jnp.exp(m - m_new)
            p = jnp.exp(s - m_new)
            l_new = alpha * l + jnp.sum(p, axis=-1, keepdims=True)
            a_new = alpha * a + jnp.dot(p.astype(vt.dtype), vt, preferred_element_type=F32)
            return m_new, l_new, a_new

        m1, l1, a1 = comp(q1, kt[:, :DIFF_DH], m1, l1, a1)
        m2, l2, a2 = comp(q2, kt[:, DIFF_DH:], m2, l2, a2)
        return m1, l1, a1, m2, l2, a2

    m0 = jnp.full((tq, 1), NEG_INF, F32)
    z1 = jnp.zeros((tq, 1), F32)
    za = jnp.zeros((tq, DIFF_DV), F32)
    carry = lax.fori_loop(0, qi, functools.partial(tile, diag=False), (m0, z1, za, m0, z1, za))
    m1, l1, a1, m2, l2, a2 = tile(qi, carry, True)
    o = a1 / l1 - lam * (a2 / l2)
    on = o * lax.rsqrt(jnp.mean(o * o, axis=-1, keepdims=True) + RMS_EPS) * sg_ref[...]
    o_ref[...] = (on * (1.0 - lam_init)).astype(o_ref.dtype)


def _prompt_attention(qn, kb, vb, slopes, lam_vecs, subln_g, *, batch, seq, tq, lam_init):
    n, width = qn.shape
    nq = seq // tq
    assert seq % tq == 0 and width == DIFF_HEADS * DIFF_DV
    return pl.pallas_call(
        functools.partial(_attn_kernel, tq=tq, lam_init=lam_init),
        grid=(batch, DIFF_HEADS, nq),
        in_specs=[
            pl.BlockSpec(memory_space=pltpu.SMEM),
            pl.BlockSpec((4, DIFF_DH), lambda b, h, i: (0, 0)),
            pl.BlockSpec((tq, DIFF_DV), lambda b, h, i: (b * nq + i, h)),
            pl.BlockSpec((seq, DIFF_DV), lambda b, h, i: (b, h)),
            pl.BlockSpec((seq, DIFF_DV), lambda b, h, i: (b, h)),
            pl.BlockSpec((1, DIFF_DV), lambda b, h, i: (0, h)),
        ],
        out_specs=pl.BlockSpec((tq, DIFF_DV), lambda b, h, i: (b * nq + i, h)),
        out_shape=jax.ShapeDtypeStruct((n, width), BF16),
        compiler_params=_cparams(("parallel", "parallel", "arbitrary")),
        name="prompt_attn",
    )(slopes, lam_vecs, qn, kb, vb, subln_g)


def _mlstm_kernel(mqk_ref, mv_ref, og_ref, gcol_ref, grow_ref, mg_ref,
                  hb_ref, c_out_ref, n_out_ref, m_out_ref, c_scr, n_scr, m_scr, *, chunk):
    c = pl.program_id(1)
    nh, dk, dv = MLSTM_HEADS, MLSTM_DK, MLSTM_DV

    @pl.when(c == 0)
    def _():
        c_scr[...] = jnp.zeros_like(c_scr)
        n_scr[...] = jnp.zeros_like(n_scr)
        m_scr[...] = jnp.zeros_like(m_scr)

    row = lax.broadcasted_iota(jnp.int32, (chunk, chunk), 0)
    col = lax.broadcasted_iota(jnp.int32, (chunk, chunk), 1)
    causal = col <= row
    tril = causal.astype(F32)
    triu = (row <= col).astype(F32)
    gcol = gcol_ref[...]
    grow = grow_ref[...]
    fcol = jnp.dot(tril, gcol, precision=HIGHEST, preferred_element_type=F32)
    frow = jnp.dot(grow, triu, precision=HIGHEST, preferred_element_type=F32)
    nt = (((1,), (1,)), ((), ()))
    tn = (((0,), (0,)), ((), ()))

    for h in range(nh):
        q = mqk_ref[:, h * dk:(h + 1) * dk]
        k = mqk_ref[:, (nh + h) * dk:(nh + h + 1) * dk]
        v = mv_ref[:, h * dv:(h + 1) * dv]
        f_c = fcol[:, nh + h:nh + h + 1]
        ig_c = gcol[:, h:h + 1]
        f_r = frow[nh + h:nh + h + 1, :]
        ig_r = grow[h:h + 1, :]
        c0 = c_scr[h]
        n0 = n_scr[h:h + 1, :]
        m0 = m_scr[h:h + 1, 0:1]

        logw = jnp.where(causal, f_c + (ig_r - f_r), NEG_INF)
        inter = m0 + f_c
        m_t = jnp.maximum(inter, jnp.max(logw, axis=-1, keepdims=True))
        w = jnp.where(causal, jnp.exp(logw - m_t), 0.0)
        a_inter = jnp.exp(inter - m_t)
        qk = lax.dot_general(q, k, nt, preferred_element_type=F32) * w
        num = (a_inter * jnp.dot(q, c0.astype(q.dtype), preferred_element_type=F32)
               + jnp.dot(qk.astype(v.dtype), v, preferred_element_type=F32))
        den = (a_inter * jnp.sum(q.astype(F32) * n0, axis=-1, keepdims=True)
               + jnp.sum(qk, axis=-1, keepdims=True))
        hh = num / jnp.maximum(jnp.abs(den), jnp.exp(-m_t))

        m_end = m_t[chunk - 1:chunk, :]
        f_last = f_c[chunk - 1:chunk, :]
        w_end = jnp.exp(f_last - f_c + ig_c - m_end)
        decay = a_inter[chunk - 1:chunk, :]
        kw = k.astype(F32) * w_end
        c_new = decay * c0 + lax.dot_general(kw.astype(v.dtype), v, tn, preferred_element_type=F32)
        n_new = decay * n0 + jnp.sum(kw, axis=0, keepdims=True)
        c_scr[h] = c_new
        n_scr[h:h + 1, :] = n_new
        m_scr[h:h + 1, :] = jnp.broadcast_to(m_end, (1, LANES))

        hn = hh * lax.rsqrt(jnp.mean(hh * hh, axis=-1, keepdims=True) + RMS_EPS) * mg_ref[:, h * dv:(h + 1) * dv]
        hb_ref[:, h * dv:(h + 1) * dv] = (hn * og_ref[:, h * dv:(h + 1) * dv].astype(F32)).astype(hb_ref.dtype)

    @pl.when(c == pl.num_programs(1) - 1)
    def _():
        c_out_ref[0] = c_scr[...]
        n_out_ref[0] = n_scr[...]
        m_out_ref[0] = m_scr[...]


def _prompt_mlstm(mqk, mv, og, gcol, grow, mnorm_g, *, batch, seq, chunk):
    n = mqk.shape[0]
    nc = seq // chunk
    nh, dk, dv = MLSTM_HEADS, MLSTM_DK, MLSTM_DV
    assert seq % chunk == 0
    blk = lambda b, c: (b * nc + c, 0)
    state = lambda b, c: (b, 0, 0)
    return pl.pallas_call(
        functools.partial(_mlstm_kernel, chunk=chunk),
        grid=(batch, nc),
        in_specs=[
            pl.BlockSpec((chunk, 2 * nh * dk), blk),
            pl.BlockSpec((chunk, nh * dv), blk),
            pl.BlockSpec((chunk, nh * dv), blk),
            pl.BlockSpec((chunk, LANES), blk),
            pl.BlockSpec((8, chunk), lambda b, c: (0, b * nc + c)),
            pl.BlockSpec((1, nh * dv), lambda b, c: (0, 0)),
        ],
        out_specs=(
            pl.BlockSpec((chunk, nh * dv), blk),
            pl.BlockSpec((1, nh, dk, dv), lambda b, c: (b, 0, 0, 0)),
            pl.BlockSpec((1, nh, dk), state),
            pl.BlockSpec((1, nh, LANES), state),
        ),
        out_shape=(
            jax.ShapeDtypeStruct((n, nh * dv), BF16),
            jax.ShapeDtypeStruct((batch, nh, dk, dv), F32),
            jax.ShapeDtypeStruct((batch, nh, dk), F32),
            jax.ShapeDtypeStruct((batch, nh, LANES), F32),
        ),
        scratch_shapes=[pltpu.VMEM((nh, dk, dv), F32), pltpu.VMEM((nh, dk), F32), pltpu.VMEM((nh, LANES), F32)],
        compiler_params=_cparams(("parallel", "arbitrary")),
        name="prompt_mlstm",
    )(mqk, mv, og, gcol, grow, mnorm_g)


def _merge_kernel(x_ref, a_ref, hb_ref, gates_ref, wa_ref, wb_ref, wo_ref, g2_ref, wqt_ref,
                  x2_ref, xnt_ref, pqt_ref):
    width = a_ref.shape[1]
    ya = jnp.dot(a_ref[...], wa_ref[...], preferred_element_type=F32)
    yb = jnp.dot(hb_ref[...], wb_ref[...], preferred_element_type=F32)
    y = gates_ref[:, :width].astype(F32) * ya + gates_ref[:, width:].astype(F32) * yb
    x2 = x_ref[...] + jnp.dot(y.astype(BF16), wo_ref[...], preferred_element_type=F32)
    x2_ref[...] = x2
    xn = x2 * lax.rsqrt(jnp.mean(x2 * x2, axis=-1, keepdims=True) + RMS_EPS) * g2_ref[...]
    xnt = jnp.transpose(xn).astype(BF16)
    xnt_ref[...] = xnt
    pqt_ref[...] = jnp.dot(wqt_ref[...], xnt, preferred_element_type=F32)


def _merge(x2d, a, hb, gates, w_a, w_b, w_o, norm2_g, wq_t, *, tm):
    n, d = x2d.shape
    nq = wq_t.shape[0]
    row = lambda i: (i, 0)
    const = lambda i: (0, 0)
    colb = lambda i: (0, i)
    return pl.pallas_call(
        _merge_kernel,
        grid=(n // tm,),
        in_specs=[
            pl.BlockSpec((tm, d), row),
            pl.BlockSpec((tm, a.shape[1]), row),
            pl.BlockSpec((tm, hb.shape[1]), row),
            pl.BlockSpec((tm, gates.shape[1]), row),
            pl.BlockSpec(w_a.shape, const),
            pl.BlockSpec(w_b.shape, const),
            pl.BlockSpec(w_o.shape, const),
            pl.BlockSpec((1, d), const),
            pl.BlockSpec(wq_t.shape, const),
        ],
        out_specs=(pl.BlockSpec((tm, d), row), pl.BlockSpec((d, tm), colb), pl.BlockSpec((nq, tm), colb)),
        out_shape=(
            jax.ShapeDtypeStruct((n, d), F32),
            jax.ShapeDtypeStruct((d, n), BF16),
            jax.ShapeDtypeStruct((nq, n), F32),
        ),
        compiler_params=_cparams(("parallel",)),
        name="merge",
    )(x2d, a, hb, gates, w_a, w_b, w_o, norm2_g, wq_t)


def _topk_rows(s, k):
    n = s.shape[0]
    idx = lax.broadcasted_iota(jnp.int32, s.shape, 0)
    rank = jnp.full(s.shape, float(k), F32)
    vals = []
    for r in range(k):
        m = jnp.max(s, axis=0, keepdims=True)
        first = jnp.min(jnp.where(s == m, idx, n), axis=0, keepdims=True)
        hit = idx == first
        rank = jnp.where(hit, float(r), rank)
        vals.append(m)
        s = jnp.where(hit, -jnp.inf, s)
    return rank, vals


def _peer_topk_kernel(pqt_ref, sk_ref, lrow_ref, rank2_ref, e1_ref, e2_ref):
    k = PEER_TOPK
    nk = PEER_NKEYS
    tn = pqt_ref.shape[1]
    pairs = [(r, s) for r in range(k) for s in range(k // (r + 1))]
    n_pad = -len(pairs) % 8

    def head(h, _):
        base = pl.multiple_of(h * 2 * PEER_HALF, 2 * PEER_HALF)
        out = pl.multiple_of(h * nk, nk)
        q1 = pqt_ref[pl.ds(base, PEER_HALF), :]
        q2 = pqt_ref[pl.ds(base + PEER_HALF, PEER_HALF), :]
        s1 = jnp.dot(sk_ref[0], q1, precision=HIGHEST, preferred_element_type=F32)
        s2 = jnp.dot(sk_ref[1], q2, precision=HIGHEST, preferred_element_type=F32)
        rank1, v1 = _topk_rows(s1, k)
        rank2, v2 = _topk_rows(s2, k)
        cand = jnp.concatenate([v1[r] + v2[s] for r, s in pairs]
                               + [jnp.full((n_pad, tn), -jnp.inf, F32)], axis=0)
        crank, _ = _topk_rows(cand, k)
        chosen = crank < float(k)
        e1v = [jnp.exp(v - v1[0]) for v in v1]
        e2v = [jnp.exp(v - v2[0]) for v in v2]
        ecand = jnp.concatenate([e1v[r] * e2v[s] for r, s in pairs]
                                + [jnp.zeros((n_pad, tn), F32)], axis=0)
        z = jnp.sum(jnp.where(chosen, ecand, 0.0), axis=0, keepdims=True)
        cnt = chosen.astype(F32)
        lrow = jnp.zeros((nk, tn), F32)
        pos = 0
        for r in range(k):
            n_r = k // (r + 1)
            l_r = jnp.sum(cnt[pos:pos + n_r, :], axis=0, keepdims=True)
            pos += n_r
            lrow = jnp.where(rank1 == float(r), l_r, lrow)
        lrow_ref[pl.ds(out, nk), :] = lrow
        rank2_ref[pl.ds(out, nk), :] = rank2
        e1_ref[pl.ds(out, nk), :] = jnp.where(rank1 < float(k), jnp.exp(s1 - v1[0]), 0.0) / z
        e2_ref[pl.ds(out, nk), :] = jnp.where(rank2 < float(k), jnp.exp(s2 - v2[0]), 0.0)
        return 0

    lax.fori_loop(0, PEER_HEADS, head, 0)


def _peer_topk(pq_t, subkeys, *, tn):
    nq, n = pq_t.shape
    rows = PEER_HEADS * PEER_NKEYS
    spec = pl.BlockSpec((rows, tn), lambda t: (0, t))
    shp = jax.ShapeDtypeStruct((rows, n), F32)
    return pl.pallas_call(
        _peer_topk_kernel,
        grid=(n // tn,),
        in_specs=[pl.BlockSpec((nq, tn), lambda t: (0, t)),
                  pl.BlockSpec(subkeys.shape, lambda t: (0, 0, 0))],
        out_specs=(spec,) * 4,
        out_shape=(shp,) * 4,
        compiler_params=_cparams(("parallel",)),
        name="peer_topk",
    )(pq_t, subkeys)


def _gelu_tanh(x):
    return 0.5 * x * (1.0 + jnp.tanh(math.sqrt(2.0 / math.pi) * (x + 0.044715 * (x * x * x))))


def _peer_expert_kernel(xnt_ref, u_ref, vt_ref, lrow_ref, rank2_ref, e1_ref, e2_ref, x2_ref,
                        y_ref, acc_scr, wc_scr):
    e = pl.program_id(1)
    nk = PEER_NKEYS
    sub = u_ref.shape[0] // nk

    @pl.when(e == 0)
    def _():
        acc_scr[...] = jnp.zeros_like(acc_scr)

    act = jnp.dot(u_ref[...], xnt_ref[...], preferred_element_type=F32)
    for ii in range(sub):
        i1 = e * sub + ii
        w = jnp.zeros((nk, act.shape[1]), F32)
        for h in range(PEER_HEADS):
            lr = lrow_ref[pl.ds(h * nk + i1, 1), :]
            e1 = e1_ref[pl.ds(h * nk + i1, 1), :]
            r2 = rank2_ref[h * nk:(h + 1) * nk, :]
            e2 = e2_ref[h * nk:(h + 1) * nk, :]
            w = w + jnp.where(r2 < lr, e2 * e1, 0.0)
        wc_scr[ii * nk:(ii + 1) * nk, :] = (w * _gelu_tanh(act[ii * nk:(ii + 1) * nk, :])).astype(BF16)
    acc_scr[...] += jnp.dot(vt_ref[...], wc_scr[...], preferred_element_type=F32)

    @pl.when(e == pl.num_programs(1) - 1)
    def _():
        y_ref[...] = x2_ref[...] + jnp.transpose(acc_scr[...])


def _peer_experts(xn_t, u_bf, v_t, lrow, rank2, e1, e2, x2, *, tn, te):
    d, n = xn_t.shape
    n_exp = u_bf.shape[0]
    rows = lrow.shape[0]
    tok = lambda t, e: (0, t)
    return pl.pallas_call(
        _peer_expert_kernel,
        grid=(n // tn, n_exp // te),
        in_specs=[
            pl.BlockSpec((d, tn), tok),
            pl.BlockSpec((te, d), lambda t, e: (e, 0)),
            pl.BlockSpec((d, te), lambda t, e: (0, e)),
            pl.BlockSpec((rows, tn), tok),
            pl.BlockSpec((rows, tn), tok),
            pl.BlockSpec((rows, tn), tok),
            pl.BlockSpec((rows, tn), tok),
            pl.BlockSpec((tn, d), lambda t, e: (t, 0)),
        ],
        out_specs=pl.BlockSpec((tn, d), lambda t, e: (t, 0)),
        out_shape=jax.ShapeDtypeStruct((n, d), F32),
        scratch_shapes=[pltpu.VMEM((d, tn), F32), pltpu.VMEM((te, tn), BF16)],
        compiler_params=_cparams(("parallel", "arbitrary")),
        name="peer_experts",
    )(xn_t, u_bf, v_t, lrow, rank2, e1, e2, x2)


def _seg_dot(p, g_ref):
    hi = p.astype(BF16)
    lo = (p - hi.astype(F32)).astype(BF16)
    return (jnp.dot(hi, g_ref[...], preferred_element_type=F32)
            + jnp.dot(lo, g_ref[...], preferred_element_type=F32))


def _decode_attn_kernel(pt_ref, q_ref, kn_ref, vn_ref, kc_ref, vc_ref, g_ref, e0_ref, e1_ref,
                        srow_ref, lam_ref, sg_ref, o_ref, s_scr, acc_scr, anew_scr, *, n_pages, lam_init):
    j = pl.program_id(1)
    ps = PAGE_SIZE
    past = n_pages * ps
    lam = _diff_lambda(lam_ref[0:1, :], lam_ref[1:2, :], lam_ref[2:3, :], lam_ref[3:4, :], lam_init)
    q = q_ref[0]

    @pl.when(j < n_pages)
    def _():
        s = _seg_dot(kc_ref[0] * q, g_ref)
        t = lax.broadcasted_iota(jnp.int32, (ps, 1), 0)
        dist = (past - (j * ps + t)).astype(F32)
        s_scr[pl.ds(pl.multiple_of(j * ps, ps), ps), :] = s - srow_ref[...] * dist

    @pl.when(j == n_pages)
    def _():
        s_new = _seg_dot(jnp.broadcast_to(kn_ref[0] * q, (8, q.shape[1])), g_ref)[0:1, :]
        s_all = s_scr[...]
        m = jnp.maximum(jnp.max(s_all, axis=0, keepdims=True), s_new)
        p = jnp.exp(s_all - m)
        p_new = jnp.exp(s_new - m)
        inv_l = 1.0 / (jnp.sum(p, axis=0, keepdims=True) + p_new)
        s_scr[...] = p * inv_l
        anew_scr[...] = jnp.broadcast_to(p_new * inv_l, anew_scr.shape)
        acc_scr[...] = jnp.zeros_like(acc_scr)

    def expand(pn):
        pb = pn.astype(BF16)
        return (jnp.dot(pb, e0_ref[...], preferred_element_type=F32)
                - lam * jnp.dot(pb, e1_ref[...], preferred_element_type=F32))

    @pl.when(j >= n_pages)
    def _():
        jj = j - n_pages
        a = expand(s_scr[pl.ds(pl.multiple_of(jj * ps, ps), ps), :])
        prod = a * vc_ref[0]
        acc_scr[...] += jnp.sum(prod.reshape(ps // 8, 8, prod.shape[1]), axis=0)

    @pl.when(j == 2 * n_pages - 1)
    def _():
        o = jnp.sum(acc_scr[...], axis=0, keepdims=True) + expand(anew_scr[...])[0:1, :] * vn_ref[0]
        for h in range(DIFF_HEADS):
            oh = o[:, h * DIFF_DV:(h + 1) * DIFF_DV]
            on = oh * lax.rsqrt(jnp.mean(oh * oh, axis=-1, keepdims=True) + RMS_EPS)
            o_ref[0, :, h * DIFF_DV:(h + 1) * DIFF_DV] = on * sg_ref[:, h * DIFF_DV:(h + 1) * DIFF_DV] * (1.0 - lam_init)


def _decode_attention(page_table, q, k_new, v_new, cache_k, cache_v, lam_vecs, subln_g, *, lam_init):
    bs, n_pages = page_table.shape
    n_pool, ps, nh, dv = cache_v.shape
    width = nh * dv
    assert ps == PAGE_SIZE and cache_k.shape == cache_v.shape
    kc = cache_k.reshape(n_pool, ps, width)
    vc = cache_v.reshape(n_pool, ps, width)
    lane = jnp.arange(width, dtype=jnp.int32)
    comp = jnp.arange(LANES, dtype=jnp.int32)
    gmat = (lane[:, None] // DIFF_DH == comp[None, :]).astype(BF16)
    e0 = (comp[:, None] == 2 * (lane[None, :] // dv)).astype(BF16)
    e1 = (comp[:, None] == 2 * (lane[None, :] // dv) + 1).astype(BF16)
    slopes = jnp.exp2(-8.0 * jnp.arange(1, nh + 1, dtype=F32) / nh)
    srow = jnp.where(comp < 2 * nh, slopes[jnp.minimum(comp // 2, nh - 1)], 0.0).reshape(1, LANES)
    row3 = lambda b, j, pt: (b, 0, 0)
    c2 = lambda b, j, pt: (0, 0)
    grid_spec = pltpu.PrefetchScalarGridSpec(
        num_scalar_prefetch=1,
        grid=(bs, 2 * n_pages),
        in_specs=[
            pl.BlockSpec((1, 1, width), row3),
            pl.BlockSpec((1, 1, width), row3),
            pl.BlockSpec((1, 1, width), row3),
            pl.BlockSpec((1, ps, width), lambda b, j, pt: (pt[b * n_pages + jnp.minimum(j, n_pages - 1)], 0, 0)),
            pl.BlockSpec((1, ps, width), lambda b, j, pt: (pt[b * n_pages + jnp.maximum(j - n_pages, 0)], 0, 0)),
            pl.BlockSpec((width, LANES), c2),
            pl.BlockSpec((LANES, width), c2),
            pl.BlockSpec((LANES, width), c2),
            pl.BlockSpec((1, LANES), c2),
            pl.BlockSpec((4, DIFF_DH), c2),
            pl.BlockSpec((1, width), c2),
        ],
        out_specs=pl.BlockSpec((1, 1, width), row3),
        scratch_shapes=[pltpu.VMEM((n_pages * ps, LANES), F32), pltpu.VMEM((8, width), F32),
                        pltpu.VMEM((8, LANES), F32)],
    )
    return pl.pallas_call(
        functools.partial(_decode_attn_kernel, n_pages=n_pages, lam_init=lam_init),
        grid_spec=grid_spec,
        out_shape=jax.ShapeDtypeStruct((bs, 1, width), F32),
        compiler_params=_cparams(("parallel", "arbitrary")),
        name="decode_attn",
    )(page_table.reshape(-1), q, k_new, v_new, kc, vc, gmat, e0, e1, srow, lam_vecs, subln_g)


def _mlstm_step_kernel(q_ref, k_ref, v_ref, og_ref, ig_ref, lf_ref, c_ref, n_ref, m_ref, mg_ref,
                       hb_ref, c_out_ref, n_out_ref, m_out_ref):
    dv = MLSTM_DV
    for h in range(MLSTM_HEADS):
        qc, kc, vr = q_ref[0, h], k_ref[0, h], v_ref[0, h]
        c0, n0, m0 = c_ref[0, h], n_ref[0, h], m_ref[0, h]
        ig, lf = ig_ref[0, h], lf_ref[0, h]
        inter = m0 + lf
        m = jnp.maximum(inter, ig)
        w = jnp.exp(ig - m)
        a_inter = jnp.exp(inter - m)
        qk = jnp.sum(qc * kc, axis=0, keepdims=True) * w
        num = a_inter * jnp.sum(c0 * qc, axis=0, keepdims=True) + qk * vr
        den = a_inter * jnp.sum(qc * n0, axis=0, keepdims=True) + qk
        hh = num / jnp.maximum(jnp.abs(den), jnp.exp(-m))
        c_out_ref[0, h] = a_inter * c0 + w * (kc * vr)
        n_out_ref[0, h] = a_inter * n0 + w * kc
        m_out_ref[0, h] = m
        hn = hh * lax.rsqrt(jnp.mean(hh * hh, axis=-1, keepdims=True) + RMS_EPS) * mg_ref[:, h * dv:(h + 1) * dv]
        hb_ref[0, :, h * dv:(h + 1) * dv] = hn * og_ref[0, :, h * dv:(h + 1) * dv]


def _mlstm_step(q, k, v, og, ig, lf, c0, n0, m0, mnorm_g):
    bs, nh, dk, dv = c0.shape
    b4 = lambda b: (b, 0, 0, 0)
    b3 = lambda b: (b, 0, 0)
    col = pl.BlockSpec((1, nh, dk, 1), b4)
    one = pl.BlockSpec((1, nh, 1, 1), b4)
    cspec = pl.BlockSpec((1, nh, dk, dv), b4)
    hspec = pl.BlockSpec((1, 1, nh * dv), b3)
    return pl.pallas_call(
        _mlstm_step_kernel,
        grid=(bs,),
        in_specs=[col, col, pl.BlockSpec((1, nh, 1, dv), b4), hspec, one, one, cspec, col, one,
                  pl.BlockSpec((1, nh * dv), lambda b: (0, 0))],
        out_specs=(hspec, cspec, col, one),
        out_shape=(
            jax.ShapeDtypeStruct((bs, 1, nh * dv), F32),
            jax.ShapeDtypeStruct((bs, nh, dk, dv), F32),
            jax.ShapeDtypeStruct((bs, nh, dk, 1), F32),
            jax.ShapeDtypeStruct((bs, nh, 1, 1), F32),
        ),
        compiler_params=_cparams(("parallel",)),
        name="mlstm_step",
    )(q, k, v, og, ig, lf, c0, n0, m0, mnorm_g)


def _row_tile(n, cap):
    t = min(n, cap)
    assert n % t == 0
    return t


def _layer(xp, xs, cache_k, cache_v, st_c, st_n, st_m, page_table, lam_init,
           norm1_g, w_in, q_norm_g, k_norm_g, lam_q1, lam_k1, lam_q2, lam_k2, subln_g, b_i, b_f,
           mnorm_g, w_a, w_b, w_o, norm2_g, peer_wq, subkeys, peer_u, peer_v):
    bp, t, d = xp.shape
    bs, ts, _ = xs.shape
    assert ts == 1
    n_p = bp * t
    nh, dk, dv = MLSTM_HEADS, MLSTM_DK, MLSTM_DV

    prep = _prep_in_proj(w_in, q_norm_g, k_norm_g, b_i, b_f)
    g1 = norm1_g.reshape(1, d).astype(F32)
    lam_vecs = jnp.stack([lam_q1, lam_k1, lam_q2, lam_k2]).astype(F32)
    sg = subln_g.reshape(1, -1).astype(F32)
    mg = mnorm_g.reshape(1, -1).astype(F32)
    slopes = jnp.exp2(-8.0 * jnp.arange(1, DIFF_HEADS + 1, dtype=F32) / DIFF_HEADS)
    wa_b, wb_b, wo_b = w_a.astype(BF16), w_b.astype(BF16), w_o.astype(BF16)
    g2 = norm2_g.reshape(1, d).astype(F32)
    wq_t = peer_wq.T.astype(BF16)
    u_b = peer_u.astype(BF16)
    v_t = peer_v.T.astype(BF16)

    def peer(x2, xn_t, pq_t, tn):
        lrow, rank2, e1, e2 = _peer_topk(pq_t, subkeys, tn=LANES)
        return _peer_experts(xn_t, u_b, v_t, lrow, rank2, e1, e2, x2, tn=tn, te=8 * PEER_NKEYS)

    x2d = xp.reshape(n_p, d)
    qn, kf, kb, vf, vb, mqk, mv, og, gates, gcol, grow = _in_proj(
        x2d, g1, *prep, tm=_row_tile(n_p, 512), act_dtype=BF16)
    a = _prompt_attention(qn, kb, vb, slopes, lam_vecs, sg, batch=bp, seq=t, tq=_row_tile(t, 256),
                          lam_init=lam_init)
    hb, c_p, n_p_, m_p = _prompt_mlstm(mqk, mv, og, gcol, grow, mg, batch=bp, seq=t, chunk=_row_tile(t, 256))
    x2, xn_t, pq_t = _merge(x2d, a, hb, gates, wa_b, wb_b, wo_b, g2, wq_t, tm=_row_tile(n_p, 256))
    y_p = peer(x2, xn_t, pq_t, _row_tile(n_p, 256)).reshape(bp, t, d)
    k_p = kf.reshape(bp, t, DIFF_HEADS, 2 * DIFF_DH)
    v_p = vf.reshape(bp, t, DIFF_HEADS, DIFF_DV)
    m_p = m_p[:, :, 0]

    xs2d = xs.reshape(bs, d)
    qn, kf, _, vf, _, mqk, mv, og, gates, gcol, _ = _in_proj(xs2d, g1, *prep, tm=bs, act_dtype=F32)
    a = _decode_attention(page_table, qn.reshape(bs, 1, -1), kf.reshape(bs, 1, -1), vf.reshape(bs, 1, -1),
                          cache_k, cache_v, lam_vecs, sg, lam_init=lam_init)
    hb, c_s, n_s, m_s = _mlstm_step(
        mqk[:, :nh * dk].reshape(bs, nh, dk, 1), mqk[:, nh * dk:].reshape(bs, nh, dk, 1),
        mv.reshape(bs, nh, 1, dv), og.reshape(bs, 1, nh * dv),
        gcol[:, :nh].reshape(bs, nh, 1, 1), gcol[:, nh:2 * nh].reshape(bs, nh, 1, 1),
        st_c.astype(F32), st_n.astype(F32).reshape(bs, nh, dk, 1), st_m.astype(F32).reshape(bs, nh, 1, 1), mg)
    pad = -bs % LANES
    padr = lambda z: jnp.pad(z, ((0, pad), (0, 0)))
    x2, xn_t, pq_t = _merge(padr(xs2d), padr(a.reshape(bs, -1)).astype(BF16), padr(hb.reshape(bs, -1)).astype(BF16),
                            padr(gates).astype(BF16), wa_b, wb_b, wo_b, g2, wq_t, tm=LANES)
    y_s = peer(x2, xn_t, pq_t, LANES)[:bs].reshape(bs, 1, d)
    k_s = kf.reshape(bs, 1, DIFF_HEADS, 2 * DIFF_DH)
    v_s = vf.reshape(bs, 1, DIFF_HEADS, DIFF_DV)
    return (y_p, y_s, k_p, v_p, c_p, n_p_, m_p, k_s, v_s, c_s, n_s.reshape(bs, nh, dk), m_s.reshape(bs, nh))


def kernel(x_prompt, x_sample, cache_k, cache_v, state_C, state_n, state_m, page_table, norm1_g, w_in,
           q_norm_g, k_norm_g, lam_q1, lam_k1, lam_q2, lam_k2, diff_subln_g, b_i, b_f, mlstm_norm_g,
           w_branch_a, w_branch_b, w_out, norm2_g, peer_wq, peer_subkeys, peer_u, peer_v):
    depth = w_in.shape[0]
    xp, xs = x_prompt, x_sample
    per_layer = []
    for l in range(depth):
        lam_init = 0.8 - 0.6 * math.exp(-0.3 * l)
        outs = _layer(xp, xs, cache_k[l], cache_v[l], state_C[l], state_n[l], state_m[l], page_table, lam_init,
                      norm1_g[l], w_in[l], q_norm_g[l], k_norm_g[l], lam_q1[l], lam_k1[l], lam_q2[l], lam_k2[l],
                      diff_subln_g[l], b_i[l], b_f[l], mlstm_norm_g[l], w_branch_a[l], w_branch_b[l], w_out[l],
                      norm2_g[l], peer_wq[l], peer_subkeys[l], peer_u[l], peer_v[l])
        xp, xs = outs[0], outs[1]
        per_layer.append(outs[2:])
    if depth == 1:
        stacked = tuple(o[None] for o in per_layer[0])
    else:
        stacked = tuple(jnp.stack([p[i] for p in per_layer]) for i in range(10))
    return (xp, xs) + stacked
```

```python
import functools
import math

import jax
import jax.numpy as jnp
from jax import lax
from jax.experimental import pallas as pl
from jax.experimental.pallas import tpu as pltpu

F32 = jnp.float32
BF16 = jnp.bfloat16
HIGHEST = lax.Precision.HIGHEST

RMS_EPS = 1e-6
NEG_INF = -1e30

DIFF_HEADS = 8
DIFF_DH = 64
DIFF_DV = 2 * DIFF_DH
MLSTM_HEADS = 4
MLSTM_DK = 128
MLSTM_DV = 256
PEER_HEADS = 8
PEER_NKEYS = 128
PEER_HALF = 128
PEER_TOPK = 16
PAGE_SIZE = 128

LANES = 128
COL_BLOCK = 1024
VMEM_LIMIT = 48 * 1024 * 1024


def _cparams(sem):
    return pltpu.CompilerParams(dimension_semantics=sem, vmem_limit_bytes=VMEM_LIMIT)


def _log_sigmoid(x):
    return jnp.minimum(x, 0.0) - jnp.log1p(jnp.exp(-jnp.abs(x)))


def _sigmoid(x):
    return 1.0 / (1.0 + jnp.exp(-x))


def _in_proj_kernel(x_ref, g1_ref, w_ref, wg_ref, qg_ref, kg_ref, gsum_ref, brow_ref, cs_ref,
                    qn_ref, kf_ref, kb_ref, vf_ref, vb_ref, mqk_ref, mv_ref, og_ref, gates_ref,
                    gcol_ref, grow_ref, xn_scr):
    j = pl.program_id(1)

    @pl.when(j == 0)
    def _():
        x = x_ref[...]
        xn = x * lax.rsqrt(jnp.mean(x * x, axis=-1, keepdims=True) + RMS_EPS) * g1_ref[...]
        xn_scr[...] = xn.astype(BF16)
        zc = jnp.dot(xn, wg_ref[...], precision=HIGHEST, preferred_element_type=F32) + brow_ref[...]
        lane = lax.broadcasted_iota(jnp.int32, zc.shape, 1)
        gc = jnp.where(lane < MLSTM_HEADS, zc, _log_sigmoid(zc))
        gcol_ref[...] = gc
        grow_ref[...] = jnp.transpose(gc)[:8, :]

    z = jnp.dot(xn_scr[...], w_ref[...], preferred_element_type=F32)

    def head_norm(gain):
        ss = jnp.dot((z * z).astype(BF16), gsum_ref[...], preferred_element_type=F32)
        return z * lax.rsqrt(ss * (1.0 / DIFF_DH) + RMS_EPS) * gain

    @pl.when(j == 0)
    def _():
        qn_ref[...] = head_norm(qg_ref[...]).astype(qn_ref.dtype)

    @pl.when(j == 1)
    def _():
        kn = head_norm(kg_ref[...])
        kf_ref[...] = kn
        kb_ref[...] = kn.astype(kb_ref.dtype)

    @pl.when(j == 2)
    def _():
        vf_ref[...] = z
        vb_ref[...] = z.astype(vb_ref.dtype)

    @pl.when(j == 3)
    def _():
        mqk_ref[...] = (z * cs_ref[...]).astype(mqk_ref.dtype)

    @pl.when(j == 4)
    def _():
        mv_ref[...] = z.astype(mv_ref.dtype)

    @pl.when(j == 5)
    def _():
        og_ref[...] = _sigmoid(z).astype(og_ref.dtype)

    @pl.when(j >= 6)
    def _():
        gates_ref[...] = _sigmoid(z).astype(gates_ref.dtype)


def _in_proj(x2d, norm1_g, w_main, w_gate, qg_t, kg_t, gsum, brow, colscale, *, tm, act_dtype):
    n, d = x2d.shape
    nblk = w_main.shape[1] // COL_BLOCK
    assert nblk == 8 and n % tm == 0
    row_blk = lambda i, j: (i, 0)
    const = lambda i, j: (0, 0)
    wide = pl.BlockSpec((tm, COL_BLOCK), row_blk)
    out_shape = (
        jax.ShapeDtypeStruct((n, COL_BLOCK), act_dtype),
        jax.ShapeDtypeStruct((n, COL_BLOCK), F32),
        jax.ShapeDtypeStruct((n, COL_BLOCK), act_dtype),
        jax.ShapeDtypeStruct((n, COL_BLOCK), F32),
        jax.ShapeDtypeStruct((n, COL_BLOCK), act_dtype),
        jax.ShapeDtypeStruct((n, COL_BLOCK), act_dtype),
        jax.ShapeDtypeStruct((n, COL_BLOCK), act_dtype),
        jax.ShapeDtypeStruct((n, COL_BLOCK), act_dtype),
        jax.ShapeDtypeStruct((n, 2 * COL_BLOCK), act_dtype),
        jax.ShapeDtypeStruct((n, LANES), F32),
        jax.ShapeDtypeStruct((8, n), F32),
    )
    out_specs = (wide,) * 8 + (
        pl.BlockSpec((tm, COL_BLOCK), lambda i, j: (i, jnp.maximum(j - 6, 0))),
        pl.BlockSpec((tm, LANES), row_blk),
        pl.BlockSpec((8, tm), lambda i, j: (0, i)),
    )
    return pl.pallas_call(
        _in_proj_kernel,
        grid=(n // tm, nblk),
        in_specs=[
            pl.BlockSpec((tm, d), row_blk),
            pl.BlockSpec((1, d), const),
            pl.BlockSpec((d, COL_BLOCK), lambda i, j: (0, j)),
            pl.BlockSpec((d, LANES), const),
            pl.BlockSpec((1, COL_BLOCK), const),
            pl.BlockSpec((1, COL_BLOCK), const),
            pl.BlockSpec((COL_BLOCK, COL_BLOCK), const),
            pl.BlockSpec((1, LANES), const),
            pl.BlockSpec((1, COL_BLOCK), const),
        ],
        out_specs=out_specs,
        out_shape=out_shape,
        scratch_shapes=[pltpu.VMEM((tm, d), BF16)],
        compiler_params=_cparams(("parallel", "arbitrary")),
        name="in_proj",
    )(x2d, norm1_g, w_main, w_gate, qg_t, kg_t, gsum, brow, colscale)


def _prep_in_proj(w_in, q_norm_g, k_norm_g, b_i, b_f):
    d = w_in.shape[0]
    n_main = 3 * COL_BLOCK + 2 * MLSTM_HEADS * MLSTM_DK + 2 * MLSTM_HEADS * MLSTM_DV
    n_gate = 2 * MLSTM_HEADS
    assert w_in.shape[1] == n_main + n_gate + 2 * d and n_main == 6 * COL_BLOCK
    w_main = jnp.concatenate([w_in[:, :n_main], w_in[:, n_main + n_gate:]], axis=1).astype(BF16)
    w_gate = jnp.pad(w_in[:, n_main:n_main + n_gate], ((0, 0), (0, LANES - n_gate)))
    qg_t = jnp.tile(q_norm_g.astype(F32), COL_BLOCK // DIFF_DH).reshape(1, COL_BLOCK)
    kg_t = jnp.tile(k_norm_g.astype(F32), COL_BLOCK // DIFF_DH).reshape(1, COL_BLOCK)
    grp = jnp.arange(COL_BLOCK, dtype=jnp.int32) // DIFF_DH
    gsum = (grp[:, None] == grp[None, :]).astype(BF16)
    brow = jnp.pad(jnp.concatenate([b_i, b_f]).astype(F32), (0, LANES - n_gate)).reshape(1, LANES)
    half = MLSTM_HEADS * MLSTM_DK
    colscale = jnp.concatenate([jnp.ones((half,), F32), jnp.full((half,), MLSTM_DK ** -0.5, F32)]).reshape(1, COL_BLOCK)
    return w_main, w_gate, qg_t, kg_t, gsum, brow, colscale


def _diff_lambda(lq1, lk1, lq2, lk2, lam_init):
    e1 = jnp.exp(jnp.sum(lq1 * lk1, axis=-1, keepdims=True))
    e2 = jnp.exp(jnp.sum(lq2 * lk2, axis=-1, keepdims=True))
    return e1 - e2 + lam_init


LOG2E = 1.4426950408889634
ATTN_FAST_BOUND = 50.0
ATTN_BUILD_ROWS = 512
ATTN_GROUP = 4


def _split3(x):
    hi = x.astype(BF16).astype(F32)
    mid = (x - hi).astype(BF16).astype(F32)
    return hi, mid, x - hi - mid


def _feature_lanes(lane, base, first, second):
    out = jnp.zeros_like(lane, dtype=F32)
    for i in range(3):
        out = jnp.where(lane == base + i, first[i], out)
        out = jnp.where(lane == base + 3 + i, second[i], out)
    return out


def _attn_kernel(slopes_ref, lam_ref, q_ref, k_ref, v_ref, sg_ref, o_ref,
                 ka1_scr, ka2_scr, va_scr, kmax_scr, acc_scr, *, tq, lam_init):
    h = pl.program_id(1)
    qi = pl.program_id(2)
    seq = k_ref.shape[0]
    dh, dv = DIFF_DH, DIFF_DV
    slope2 = slopes_ref[h] * LOG2E
    lam = _diff_lambda(lam_ref[0:1, :], lam_ref[1:2, :], lam_ref[2:3, :], lam_ref[3:4, :], lam_init)
    nt = (((1,), (1,)), ((), ()))
    one3 = (1.0, 1.0, 1.0)

    @pl.when(qi == 0)
    def _():
        rows = min(ATTN_BUILD_ROWS, seq)
        lane = lax.broadcasted_iota(jnp.int32, (rows, dv), 1)

        def build(c, carry):
            k1m, k2m = carry
            start = pl.multiple_of(c * rows, rows)
            kf = k_ref[pl.ds(start, rows), :].astype(F32)
            pos = (start + lax.broadcasted_iota(jnp.int32, (rows, 1), 0)).astype(F32)
            cp = _split3(slope2 * pos)
            first = lane < dh
            ka1_scr[pl.ds(start, rows), :] = jnp.where(first, kf, _feature_lanes(lane, dh, cp, one3)).astype(BF16)
            ka2_scr[pl.ds(start, rows), :] = jnp.where(first, _feature_lanes(lane, 0, cp, one3), kf).astype(BF16)
            va_scr[pl.ds(start, rows), :] = jnp.concatenate(
                [v_ref[pl.ds(start, rows), :], jnp.ones((rows, dv), va_scr.dtype)], axis=1)
            ksq = kf * kf
            k1m = jnp.maximum(k1m, jnp.max(jnp.sum(jnp.where(first, ksq, 0.0), axis=-1, keepdims=True), axis=0, keepdims=True))
            k2m = jnp.maximum(k2m, jnp.max(jnp.sum(jnp.where(first, 0.0, ksq), axis=-1, keepdims=True), axis=0, keepdims=True))
            return k1m, k2m

        z = jnp.zeros((1, 1), F32)
        k1m, k2m = lax.fori_loop(0, seq // rows, build, (z, z))
        kmax_scr[0:1, :] = jnp.broadcast_to(jnp.sqrt(k1m), (1, LANES))
        kmax_scr[1:2, :] = jnp.broadcast_to(jnp.sqrt(k2m), (1, LANES))

    lane = lax.broadcasted_iota(jnp.int32, (tq, dv), 1)
    first = lane < dh
    qf = q_ref[...].astype(F32)
    qsq = qf * qf
    qpos = (qi * tq + lax.broadcasted_iota(jnp.int32, (tq, 1), 0)).astype(F32)
    n1 = jnp.sqrt(jnp.sum(jnp.where(first, qsq, 0.0), axis=-1, keepdims=True))
    n2 = jnp.sqrt(jnp.sum(jnp.where(first, 0.0, qsq), axis=-1, keepdims=True))
    b1 = n1 * kmax_scr[0:1, 0:1] * 1.001 + 1e-3
    b2 = n2 * kmax_scr[1:2, 0:1] * 1.001 + 1e-3
    qa1 = jnp.where(first, qf, _feature_lanes(lane, dh, one3, _split3(-(slope2 * qpos + b1)))).astype(BF16)
    qa2 = jnp.where(first, _feature_lanes(lane, 0, one3, _split3(-(slope2 * qpos + b2))), qf).astype(BF16)
    row = lax.broadcasted_iota(jnp.int32, (tq, tq), 0)
    col = lax.broadcasted_iota(jnp.int32, (tq, tq), 1)
    keep = col <= row

    def scores(j, n=1):
        start = pl.multiple_of(j * tq, tq)
        s1 = lax.dot_general(qa1, ka1_scr[pl.ds(start, n * tq), :], nt, preferred_element_type=F32)
        s2 = lax.dot_general(qa2, ka2_scr[pl.ds(start, n * tq), :], nt, preferred_element_type=F32)
        return start, s1, s2

    def fast_path():
        def tile(j, n, diag=False):
            start, s1, s2 = scores(j, n)
            p1, p2 = jnp.exp2(s1), jnp.exp2(s2)
            if diag:
                p1, p2 = jnp.where(keep, p1, 0.0), jnp.where(keep, p2, 0.0)
            pp = jnp.concatenate([p1, p2], axis=0).astype(BF16)
            return jnp.dot(pp, va_scr[pl.ds(start, n * tq), :], preferred_element_type=F32)

        acc_scr[...] = tile(qi, 1, True)

        def body(i, _):
            acc_scr[...] += tile(i * ATTN_GROUP, ATTN_GROUP)
            return 0

        lax.fori_loop(0, qi // ATTN_GROUP, body, 0)
        n = ATTN_GROUP // 2
        while n >= 1:
            @pl.when(qi % (2 * n) >= n)
            def _(n=n):
                acc_scr[...] += tile((qi // (2 * n)) * 2 * n, n)
            n //= 2
        a1, a2 = acc_scr[0:tq, :], acc_scr[tq:2 * tq, :]
        return a1[:, :dv] / a1[:, dv:dv + 1] - lam * (a2[:, :dv] / a2[:, dv:dv + 1])

    def slow_path():
        def tile(j, carry, diag):
            start, s1, s2 = scores(j)
            vt = va_scr[pl.ds(start, tq), 0:dv]
            out = []
            for s, (m, l, a) in zip((s1, s2), (carry[:3], carry[3:])):
                if diag:
                    s = jnp.where(keep, s, NEG_INF)
                m_new = jnp.maximum(m, jnp.max(s, axis=-1, keepdims=True))
                alpha = jnp.exp2(m - m_new)
                p = jnp.exp2(s - m_new)
                out += [m_new, alpha * l + jnp.sum(p, axis=-1, keepdims=True),
                        alpha * a + jnp.dot(p.astype(BF16), vt, preferred_element_type=F32)]
            return tuple(out)

        m0 = jnp.full((tq, 1), NEG_INF, F32)
        z1 = jnp.zeros((tq, 1), F32)
        za = jnp.zeros((tq, dv), F32)
        carry = lax.fori_loop(0, qi, functools.partial(tile, diag=False), (m0, z1, za, m0, z1, za))
        m1, l1, a1, m2, l2, a2 = tile(qi, carry, True)
        return a1 / l1 - lam * (a2 / l2)

    small = jnp.max(jnp.maximum(b1, b2)) < ATTN_FAST_BOUND
    o = lax.cond(small, fast_path, slow_path)
    on = o * lax.rsqrt(jnp.mean(o * o, axis=-1, keepdims=True) + RMS_EPS) * sg_ref[...]
    o_ref[...] = (on * (1.0 - lam_init)).astype(o_ref.dtype)


def _prompt_attention(qn, kb, vb, slopes, lam_vecs, subln_g, *, batch, seq, tq, lam_init):
    n, width = qn.shape
    nq = seq // tq
    assert seq % tq == 0 and width == DIFF_HEADS * DIFF_DV and seq % min(ATTN_BUILD_ROWS, seq) == 0
    return pl.pallas_call(
        functools.partial(_attn_kernel, tq=tq, lam_init=lam_init),
        grid=(batch, DIFF_HEADS, nq),
        in_specs=[
            pl.BlockSpec(memory_space=pltpu.SMEM),
            pl.BlockSpec((4, DIFF_DH), lambda b, h, i: (0, 0)),
            pl.BlockSpec((tq, DIFF_DV), lambda b, h, i: (b * nq + i, h)),
            pl.BlockSpec((seq, DIFF_DV), lambda b, h, i: (b, h)),
            pl.BlockSpec((seq, DIFF_DV), lambda b, h, i: (b, h)),
            pl.BlockSpec((1, DIFF_DV), lambda b, h, i: (0, h)),
        ],
        out_specs=pl.BlockSpec((tq, DIFF_DV), lambda b, h, i: (b * nq + i, h)),
        out_shape=jax.ShapeDtypeStruct((n, width), BF16),
        scratch_shapes=[
            pltpu.VMEM((seq, DIFF_DV), BF16), pltpu.VMEM((seq, DIFF_DV), BF16),
            pltpu.VMEM((seq, 2 * DIFF_DV), BF16), pltpu.VMEM((8, LANES), F32),
            pltpu.VMEM((2 * tq, 2 * DIFF_DV), F32),
        ],
        compiler_params=_cparams(("parallel", "parallel", "arbitrary")),
        name="prompt_attn",
    )(slopes, lam_vecs, qn, kb, vb, subln_g)


def _mlstm_kernel(mqk_ref, mv_ref, og_ref, gcol_ref, grow_ref, mg_ref,
                  hb_ref, c_out_ref, n_out_ref, m_out_ref, c_scr, n_scr, m_scr, *, chunk):
    c = pl.program_id(1)
    nh, dk, dv = MLSTM_HEADS, MLSTM_DK, MLSTM_DV

    @pl.when(c == 0)
    def _():
        c_scr[...] = jnp.zeros_like(c_scr)
        n_scr[...] = jnp.zeros_like(n_scr)
        m_scr[...] = jnp.zeros_like(m_scr)

    row = lax.broadcasted_iota(jnp.int32, (chunk, chunk), 0)
    col = lax.broadcasted_iota(jnp.int32, (chunk, chunk), 1)
    causal = col <= row
    tril = causal.astype(F32)
    triu = (row <= col).astype(F32)
    gcol = gcol_ref[...]
    grow = grow_ref[...]
    fcol = jnp.dot(tril, gcol, precision=HIGHEST, preferred_element_type=F32)
    frow = jnp.dot(grow, triu, precision=HIGHEST, preferred_element_type=F32)
    nt = (((1,), (1,)), ((), ()))
    tn = (((0,), (0,)), ((), ()))

    for h in range(nh):
        q = mqk_ref[:, h * dk:(h + 1) * dk]
        k = mqk_ref[:, (nh + h) * dk:(nh + h + 1) * dk]
        v = mv_ref[:, h * dv:(h + 1) * dv]
        f_c = fcol[:, nh + h:nh + h + 1]
        ig_c = gcol[:, h:h + 1]
        f_r = frow[nh + h:nh + h + 1, :]
        ig_r = grow[h:h + 1, :]
        c0 = c_scr[h]
        n0 = n_scr[h:h + 1, :]
        m0 = m_scr[h:h + 1, 0:1]

        logw = jnp.where(causal, f_c + (ig_r - f_r), NEG_INF)
        inter = m0 + f_c
        m_t = jnp.maximum(inter, jnp.max(logw, axis=-1, keepdims=True))
        w = jnp.where(causal, jnp.exp(logw - m_t), 0.0)
        a_inter = jnp.exp(inter - m_t)
        qk = lax.dot_general(q, k, nt, preferred_element_type=F32) * w
        num = (a_inter * jnp.dot(q, c0.astype(q.dtype), preferred_element_type=F32)
               + jnp.dot(qk.astype(v.dtype), v, preferred_element_type=F32))
        den = (a_inter * jnp.sum(q.astype(F32) * n0, axis=-1, keepdims=True)
               + jnp.sum(qk, axis=-1, keepdims=True))
        hh = num / jnp.maximum(jnp.abs(den), jnp.exp(-m_t))

        m_end = m_t[chunk - 1:chunk, :]
        f_last = f_c[chunk - 1:chunk, :]
        w_end = jnp.exp(f_last - f_c + ig_c - m_end)
        decay = a_inter[chunk - 1:chunk, :]
        kw = k.astype(F32) * w_end
        c_new = decay * c0 + lax.dot_general(kw.astype(v.dtype), v, tn, preferred_element_type=F32)
        n_new = decay * n0 + jnp.sum(kw, axis=0, keepdims=True)
        c_scr[h] = c_new
        n_scr[h:h + 1, :] = n_new
        m_scr[h:h + 1, :] = jnp.broadcast_to(m_end, (1, LANES))

        hn = hh * lax.rsqrt(jnp.mean(hh * hh, axis=-1, keepdims=True) + RMS_EPS) * mg_ref[:, h * dv:(h + 1) * dv]
        hb_ref[:, h * dv:(h + 1) * dv] = (hn * og_ref[:, h * dv:(h + 1) * dv].astype(F32)).astype(hb_ref.dtype)

    @pl.when(c == pl.num_programs(1) - 1)
    def _():
        c_out_ref[0] = c_scr[...]
        n_out_ref[0] = n_scr[...]
        m_out_ref[0] = m_scr[...]


def _prompt_mlstm(mqk, mv, og, gcol, grow, mnorm_g, *, batch, seq, chunk):
    n = mqk.shape[0]
    nc = seq // chunk
    nh, dk, dv = MLSTM_HEADS, MLSTM_DK, MLSTM_DV
    assert seq % chunk == 0
    blk = lambda b, c: (b * nc + c, 0)
    state = lambda b, c: (b, 0, 0)
    return pl.pallas_call(
        functools.partial(_mlstm_kernel, chunk=chunk),
        grid=(batch, nc),
        in_specs=[
            pl.BlockSpec((chunk, 2 * nh * dk), blk),
            pl.BlockSpec((chunk, nh * dv), blk),
            pl.BlockSpec((chunk, nh * dv), blk),
            pl.BlockSpec((chunk, LANES), blk),
            pl.BlockSpec((8, chunk), lambda b, c: (0, b * nc + c)),
            pl.BlockSpec((1, nh * dv), lambda b, c: (0, 0)),
        ],
        out_specs=(
            pl.BlockSpec((chunk, nh * dv), blk),
            pl.BlockSpec((1, nh, dk, dv), lambda b, c: (b, 0, 0, 0)),
            pl.BlockSpec((1, nh, dk), state),
            pl.BlockSpec((1, nh, LANES), state),
        ),
        out_shape=(
            jax.ShapeDtypeStruct((n, nh * dv), BF16),
            jax.ShapeDtypeStruct((batch, nh, dk, dv), F32),
            jax.ShapeDtypeStruct((batch, nh, dk), F32),
            jax.ShapeDtypeStruct((batch, nh, LANES), F32),
        ),
        scratch_shapes=[pltpu.VMEM((nh, dk, dv), F32), pltpu.VMEM((nh, dk), F32), pltpu.VMEM((nh, LANES), F32)],
        compiler_params=_cparams(("parallel", "arbitrary")),
        name="prompt_mlstm",
    )(mqk, mv, og, gcol, grow, mnorm_g)


def _merge_kernel(x_ref, a_ref, hb_ref, gates_ref, wa_ref, wb_ref, wo_ref, g2_ref, wqt_ref,
                  x2_ref, xnt_ref, pqt_ref):
    width = a_ref.shape[1]
    ya = jnp.dot(a_ref[...], wa_ref[...], preferred_element_type=F32)
    yb = jnp.dot(hb_ref[...], wb_ref[...], preferred_element_type=F32)
    y = gates_ref[:, :width].astype(F32) * ya + gates_ref[:, width:].astype(F32) * yb
    x2 = x_ref[...] + jnp.dot(y.astype(BF16), wo_ref[...], preferred_element_type=F32)
    x2_ref[...] = x2
    xn = x2 * lax.rsqrt(jnp.mean(x2 * x2, axis=-1, keepdims=True) + RMS_EPS) * g2_ref[...]
    xnt = jnp.transpose(xn).astype(BF16)
    xnt_ref[...] = xnt
    pqt_ref[...] = jnp.dot(wqt_ref[...], xnt, preferred_element_type=F32)


def _merge(x2d, a, hb, gates, w_a, w_b, w_o, norm2_g, wq_t, *, tm):
    n, d = x2d.shape
    nq = wq_t.shape[0]
    row = lambda i: (i, 0)
    const = lambda i: (0, 0)
    colb = lambda i: (0, i)
    return pl.pallas_call(
        _merge_kernel,
        grid=(n // tm,),
        in_specs=[
            pl.BlockSpec((tm, d), row),
            pl.BlockSpec((tm, a.shape[1]), row),
            pl.BlockSpec((tm, hb.shape[1]), row),
            pl.BlockSpec((tm, gates.shape[1]), row),
            pl.BlockSpec(w_a.shape, const),
            pl.BlockSpec(w_b.shape, const),
            pl.BlockSpec(w_o.shape, const),
            pl.BlockSpec((1, d), const),
            pl.BlockSpec(wq_t.shape, const),
        ],
        out_specs=(pl.BlockSpec((tm, d), row), pl.BlockSpec((d, tm), colb), pl.BlockSpec((nq, tm), colb)),
        out_shape=(
            jax.ShapeDtypeStruct((n, d), F32),
            jax.ShapeDtypeStruct((d, n), BF16),
            jax.ShapeDtypeStruct((nq, n), F32),
        ),
        compiler_params=_cparams(("parallel",)),
        name="merge",
    )(x2d, a, hb, gates, w_a, w_b, w_o, norm2_g, wq_t)


def _topk_rows(s, k):
    n = s.shape[0]
    idx = lax.broadcasted_iota(jnp.int32, s.shape, 0)
    rank = jnp.full(s.shape, float(k), F32)
    vals = []
    for r in range(k):
        m = jnp.max(s, axis=0, keepdims=True)
        first = jnp.min(jnp.where(s == m, idx, n), axis=0, keepdims=True)
        hit = idx == first
        rank = jnp.where(hit, float(r), rank)
        vals.append(m)
        s = jnp.where(hit, -jnp.inf, s)
    return rank, vals


def _peer_topk_kernel(pqt_ref, sk_ref, lrow_ref, rank2_ref, e1_ref, e2_ref):
    k = PEER_TOPK
    nk = PEER_NKEYS
    tn = pqt_ref.shape[1]
    pairs = [(r, s) for r in range(k) for s in range(k // (r + 1))]
    n_pad = -len(pairs) % 8

    def head(h, _):
        base = pl.multiple_of(h * 2 * PEER_HALF, 2 * PEER_HALF)
        out = pl.multiple_of(h * nk, nk)
        q1 = pqt_ref[pl.ds(base, PEER_HALF), :]
        q2 = pqt_ref[pl.ds(base + PEER_HALF, PEER_HALF), :]
        s1 = jnp.dot(sk_ref[0], q1, precision=HIGHEST, preferred_element_type=F32)
        s2 = jnp.dot(sk_ref[1], q2, precision=HIGHEST, preferred_element_type=F32)
        rank1, v1 = _topk_rows(s1, k)
        rank2, v2 = _topk_rows(s2, k)
        cand = jnp.concatenate([v1[r] + v2[s] for r, s in pairs]
                               + [jnp.full((n_pad, tn), -jnp.inf, F32)], axis=0)
        crank, _ = _topk_rows(cand, k)
        chosen = crank < float(k)
        e1v = [jnp.exp(v - v1[0]) for v in v1]
        e2v = [jnp.exp(v - v2[0]) for v in v2]
        ecand = jnp.concatenate([e1v[r] * e2v[s] for r, s in pairs]
                                + [jnp.zeros((n_pad, tn), F32)], axis=0)
        z = jnp.sum(jnp.where(chosen, ecand, 0.0), axis=0, keepdims=True)
        cnt = chosen.astype(F32)
        lrow = jnp.zeros((nk, tn), F32)
        pos = 0
        for r in range(k):
            n_r = k // (r + 1)
            l_r = jnp.sum(cnt[pos:pos + n_r, :], axis=0, keepdims=True)
            pos += n_r
            lrow = jnp.where(rank1 == float(r), l_r, lrow)
        lrow_ref[pl.ds(out, nk), :] = lrow
        rank2_ref[pl.ds(out, nk), :] = rank2
        e1_ref[pl.ds(out, nk), :] = jnp.where(rank1 < float(k), jnp.exp(s1 - v1[0]), 0.0) / z
        e2_ref[pl.ds(out, nk), :] = jnp.where(rank2 < float(k), jnp.exp(s2 - v2[0]), 0.0)
        return 0

    lax.fori_loop(0, PEER_HEADS, head, 0)


def _peer_topk(pq_t, subkeys, *, tn):
    nq, n = pq_t.shape
    rows = PEER_HEADS * PEER_NKEYS
    spec = pl.BlockSpec((rows, tn), lambda t: (0, t))
    shp = jax.ShapeDtypeStruct((rows, n), F32)
    return pl.pallas_call(
        _peer_topk_kernel,
        grid=(n // tn,),
        in_specs=[pl.BlockSpec((nq, tn), lambda t: (0, t)),
                  pl.BlockSpec(subkeys.shape, lambda t: (0, 0, 0))],
        out_specs=(spec,) * 4,
        out_shape=(shp,) * 4,
        compiler_params=_cparams(("parallel",)),
        name="peer_topk",
    )(pq_t, subkeys)


def _gelu_tanh(x):
    return 0.5 * x * (1.0 + jnp.tanh(math.sqrt(2.0 / math.pi) * (x + 0.044715 * (x * x * x))))


def _peer_expert_kernel(xnt_ref, u_ref, vt_ref, lrow_ref, rank2_ref, e1_ref, e2_ref, x2_ref,
                        y_ref, acc_scr, wc_scr):
    e = pl.program_id(1)
    nk = PEER_NKEYS
    sub = u_ref.shape[0] // nk

    @pl.when(e == 0)
    def _():
        acc_scr[...] = jnp.zeros_like(acc_scr)

    act = jnp.dot(u_ref[...], xnt_ref[...], preferred_element_type=F32)
    for ii in range(sub):
        i1 = e * sub + ii
        w = jnp.zeros((nk, act.shape[1]), F32)
        for h in range(PEER_HEADS):
            lr = lrow_ref[pl.ds(h * nk + i1, 1), :]
            e1 = e1_ref[pl.ds(h * nk + i1, 1), :]
            r2 = rank2_ref[h * nk:(h + 1) * nk, :]
            e2 = e2_ref[h * nk:(h + 1) * nk, :]
            w = w + jnp.where(r2 < lr, e2 * e1, 0.0)
        wc_scr[ii * nk:(ii + 1) * nk, :] = (w * _gelu_tanh(act[ii * nk:(ii + 1) * nk, :])).astype(BF16)
    acc_scr[...] += jnp.dot(vt_ref[...], wc_scr[...], preferred_element_type=F32)

    @pl.when(e == pl.num_programs(1) - 1)
    def _():
        y_ref[...] = x2_ref[...] + jnp.transpose(acc_scr[...])


def _peer_experts(xn_t, u_bf, v_t, lrow, rank2, e1, e2, x2, *, tn, te):
    d, n = xn_t.shape
    n_exp = u_bf.shape[0]
    rows = lrow.shape[0]
    tok = lambda t, e: (0, t)
    return pl.pallas_call(
        _peer_expert_kernel,
        grid=(n // tn, n_exp // te),
        in_specs=[
            pl.BlockSpec((d, tn), tok),
            pl.BlockSpec((te, d), lambda t, e: (e, 0)),
            pl.BlockSpec((d, te), lambda t, e: (0, e)),
            pl.BlockSpec((rows, tn), tok),
            pl.BlockSpec((rows, tn), tok),
            pl.BlockSpec((rows, tn), tok),
            pl.BlockSpec((rows, tn), tok),
            pl.BlockSpec((tn, d), lambda t, e: (t, 0)),
        ],
        out_specs=pl.BlockSpec((tn, d), lambda t, e: (t, 0)),
        out_shape=jax.ShapeDtypeStruct((n, d), F32),
        scratch_shapes=[pltpu.VMEM((d, tn), F32), pltpu.VMEM((te, tn), BF16)],
        compiler_params=_cparams(("parallel", "arbitrary")),
        name="peer_experts",
    )(xn_t, u_bf, v_t, lrow, rank2, e1, e2, x2)


DECODE_BUFS = 4


def _decode_attn_kernel(pt_ref, w_ref, q_ref, kn_ref, vn_ref, srow_ref, lam_ref, sg_ref, kc_hbm, vc_hbm,
                        o_ref, buf, sem, s_scr, acc_scr, anew_scr, *, layer, n_pages, lam_init):
    bs = o_ref.shape[0]
    nh, dv, ps = DIFF_HEADS, DIFF_DV, PAGE_SIZE
    cols = ps * nh
    per_row = 2 * n_pages
    total = bs * per_row
    past = n_pages * ps
    lam = _diff_lambda(lam_ref[0:1, :], lam_ref[1:2, :], lam_ref[2:3, :], lam_ref[3:4, :], lam_init)
    nt = (((1,), (1,)), ((), ()))

    def page_copy(src_hbm, page, slot):
        return pltpu.make_async_copy(src_hbm.at[layer, page], buf.at[slot], sem.at[slot])

    def start(g):
        b = g // per_row
        j = g % per_row
        slot = g % DECODE_BUFS
        page = pt_ref[b * n_pages + j % n_pages]

        @pl.when(j < n_pages)
        def _():
            page_copy(kc_hbm, page, slot).start()

        @pl.when(j >= n_pages)
        def _():
            page_copy(vc_hbm, page, slot).start()

    lane_h = lax.broadcasted_iota(jnp.int32, (2 * nh, cols), 1) % nh
    row_h = lax.broadcasted_iota(jnp.int32, (2 * nh, cols), 0) % nh
    own = lane_h == row_h
    tok = (lax.broadcasted_iota(jnp.int32, (1, cols), 1) // nh).astype(F32)
    own128 = own[0:nh, 0:LANES]
    s_scr[...] = jnp.zeros_like(s_scr)

    def fold_heads(x, op):
        sh = nh
        while sh < LANES:
            x = op(x, pltpu.roll(x, sh, axis=1))
            sh *= 2
        return x

    def to_lanes(col):
        return jnp.sum(jnp.where(own128, col, 0.0), axis=0, keepdims=True)

    def tiled(x):
        return jnp.concatenate([x] * (cols // LANES), axis=1)

    def page_tiles(fn, init):
        def body(p, carry):
            off = pl.multiple_of(p * cols, cols)
            return fn(off, carry)
        return lax.fori_loop(0, n_pages, body, init)

    def normalise(b):
        prod = kn_ref[b] * q_ref[b]
        s_new = jnp.concatenate(
            [to_lanes(jnp.sum(prod[:, :DIFF_DH], axis=-1, keepdims=True)),
             to_lanes(jnp.sum(prod[:, DIFF_DH:], axis=-1, keepdims=True)),
             jnp.zeros((6, LANES), F32)], axis=0)

        def red(off, carry, op, f):
            blk = f(s_scr[:, pl.ds(off, cols)])
            for k in range(cols // LANES):
                carry = op(carry, blk[:, k * LANES:(k + 1) * LANES])
            return carry

        m = page_tiles(lambda off, c: red(off, c, jnp.maximum, lambda x: x), s_new)
        m = fold_heads(m, jnp.maximum)
        m_t = tiled(m)

        def expo(off, c):
            p = jnp.exp(s_scr[:, pl.ds(off, cols)] - m_t)
            s_scr[:, pl.ds(off, cols)] = p
            for k in range(cols // LANES):
                c = c + p[:, k * LANES:(k + 1) * LANES]
            return c

        p_new = jnp.exp(s_new - m)
        l = fold_heads(page_tiles(expo, jnp.zeros((8, LANES), F32)), jnp.add) + p_new
        inv = 1.0 / l
        inv_t = tiled(inv)

        def combine(off, c):
            pn = s_scr[:, pl.ds(off, cols)] * inv_t
            s_scr[0:1, pl.ds(off, cols)] = pn[0:1, :] - lam * pn[1:2, :]
            return c

        page_tiles(combine, 0)
        an = p_new * inv
        return an[0:1, :] - lam * an[1:2, :]

    for g0 in range(DECODE_BUFS):
        start(jnp.int32(g0))

    def step(g, _):
        b = g // per_row
        j = g % per_row
        slot = g % DECODE_BUFS
        page_copy(kc_hbm, 0, slot).wait()
        page = buf[slot].reshape(cols, dv).astype(BF16)

        @pl.when(j < n_pages)
        def _():
            r = lax.dot_general(w_ref[b], page, nt, preferred_element_type=F32)
            r = jnp.where(own, r, 0.0)
            bias = srow_ref[...] * (tok - jnp.asarray(past - j * ps, F32))
            off = pl.multiple_of(j * cols, cols)
            s_scr[0:1, pl.ds(off, cols)] = jnp.sum(r[0:nh], axis=0, keepdims=True) + bias
            s_scr[1:2, pl.ds(off, cols)] = jnp.sum(r[nh:], axis=0, keepdims=True) + bias

        @pl.when(j == n_pages)
        def _():
            acc_scr[...] = jnp.zeros_like(acc_scr)
            anew_scr[...] = jnp.broadcast_to(normalise(b), anew_scr.shape)

        @pl.when(j >= n_pages)
        def _():
            off = pl.multiple_of((j - n_pages) * cols, cols)
            a = jnp.where(own, jnp.broadcast_to(s_scr[0:1, pl.ds(off, cols)], own.shape), 0.0)
            acc_scr[...] += jnp.dot(a.astype(BF16), page, preferred_element_type=F32)

        @pl.when(j == per_row - 1)
        def _():
            a_col = jnp.sum(jnp.where(own128 & (lax.broadcasted_iota(jnp.int32, own128.shape, 1) < nh),
                                      anew_scr[...], 0.0), axis=-1, keepdims=True)
            o = acc_scr[0:nh, :] + a_col * vn_ref[b]
            on = o * lax.rsqrt(jnp.mean(o * o, axis=-1, keepdims=True) + RMS_EPS)
            o_ref[b] = on * sg_ref[...] * (1.0 - lam_init)

        @pl.when(g + DECODE_BUFS < total)
        def _():
            start(g + DECODE_BUFS)

        return 0

    lax.fori_loop(0, total, step, 0)


def _decode_attention(page_table, q, k_new, v_new, cache_k, cache_v, layer, lam_vecs, subln_g, *, lam_init):
    bs, n_pages = page_table.shape
    _, _, ps, nh, dv = cache_v.shape
    assert (ps, nh, dv) == (PAGE_SIZE, DIFF_HEADS, DIFF_DV) and cache_k.shape == cache_v.shape
    assert bs * 2 * n_pages >= DECODE_BUFS
    q8 = q.reshape(bs, nh, dv)
    half = jnp.arange(dv, dtype=jnp.int32) < DIFF_DH
    w = jnp.concatenate([jnp.where(half, q8, 0.0), jnp.where(half, 0.0, q8)], axis=1).astype(BF16)
    slopes = jnp.exp2(-8.0 * jnp.arange(1, nh + 1, dtype=F32) / nh)
    srow = jnp.tile(slopes, ps).reshape(1, ps * nh)
    vmem = pl.BlockSpec(memory_space=pltpu.VMEM)
    hbm = pl.BlockSpec(memory_space=pl.ANY)
    return pl.pallas_call(
        functools.partial(_decode_attn_kernel, layer=layer, n_pages=n_pages, lam_init=lam_init),
        in_specs=[pl.BlockSpec(memory_space=pltpu.SMEM), vmem, vmem, vmem, vmem, vmem, vmem, vmem, hbm, hbm],
        out_specs=vmem,
        out_shape=jax.ShapeDtypeStruct((bs, nh, dv), F32),
        scratch_shapes=[
            pltpu.VMEM((DECODE_BUFS, ps, nh, dv), F32),
            pltpu.SemaphoreType.DMA((DECODE_BUFS,)),
            pltpu.VMEM((8, n_pages * ps * nh), F32),
            pltpu.VMEM((2 * nh, dv), F32),
            pltpu.VMEM((nh, LANES), F32),
        ],
        compiler_params=pltpu.CompilerParams(vmem_limit_bytes=VMEM_LIMIT),
        name="decode_attn",
    )(page_table.reshape(-1), w, q8, k_new.reshape(bs, nh, dv), v_new.reshape(bs, nh, dv), srow, lam_vecs,
      subln_g.reshape(nh, dv), cache_k, cache_v)


def _mlstm_step_kernel(q_ref, k_ref, v_ref, og_ref, ig_ref, lf_ref, c_ref, n_ref, m_ref, mg_ref,
                       hb_ref, c_out_ref, n_out_ref, m_out_ref):
    dv = MLSTM_DV
    for h in range(MLSTM_HEADS):
        qc, kc, vr = q_ref[0, h], k_ref[0, h], v_ref[0, h]
        c0, n0, m0 = c_ref[0, h], n_ref[0, h], m_ref[0, h]
        ig, lf = ig_ref[0, h], lf_ref[0, h]
        inter = m0 + lf
        m = jnp.maximum(inter, ig)
        w = jnp.exp(ig - m)
        a_inter = jnp.exp(inter - m)
        qk = jnp.sum(qc * kc, axis=0, keepdims=True) * w
        num = a_inter * jnp.sum(c0 * qc, axis=0, keepdims=True) + qk * vr
        den = a_inter * jnp.sum(qc * n0, axis=0, keepdims=True) + qk
        hh = num / jnp.maximum(jnp.abs(den), jnp.exp(-m))
        c_out_ref[0, h] = a_inter * c0 + w * (kc * vr)
        n_out_ref[0, h] = a_inter * n0 + w * kc
        m_out_ref[0, h] = m
        hn = hh * lax.rsqrt(jnp.mean(hh * hh, axis=-1, keepdims=True) + RMS_EPS) * mg_ref[:, h * dv:(h + 1) * dv]
        hb_ref[0, :, h * dv:(h + 1) * dv] = hn * og_ref[0, :, h * dv:(h + 1) * dv]


def _mlstm_step(q, k, v, og, ig, lf, c0, n0, m0, mnorm_g):
    bs, nh, dk, dv = c0.shape
    b4 = lambda b: (b, 0, 0, 0)
    b3 = lambda b: (b, 0, 0)
    col = pl.BlockSpec((1, nh, dk, 1), b4)
    one = pl.BlockSpec((1, nh, 1, 1), b4)
    cspec = pl.BlockSpec((1, nh, dk, dv), b4)
    hspec = pl.BlockSpec((1, 1, nh * dv), b3)
    return pl.pallas_call(
        _mlstm_step_kernel,
        grid=(bs,),
        in_specs=[col, col, pl.BlockSpec((1, nh, 1, dv), b4), hspec, one, one, cspec, col, one,
                  pl.BlockSpec((1, nh * dv), lambda b: (0, 0))],
        out_specs=(hspec, cspec, col, one),
        out_shape=(
            jax.ShapeDtypeStruct((bs, 1, nh * dv), F32),
            jax.ShapeDtypeStruct((bs, nh, dk, dv), F32),
            jax.ShapeDtypeStruct((bs, nh, dk, 1), F32),
            jax.ShapeDtypeStruct((bs, nh, 1, 1), F32),
        ),
        compiler_params=_cparams(("parallel",)),
        name="mlstm_step",
    )(q, k, v, og, ig, lf, c0, n0, m0, mnorm_g)


def _row_tile(n, cap):
    t = min(n, cap)
    assert n % t == 0
    return t


def _layer(xp, xs, cache_k, cache_v, layer, st_c, st_n, st_m, page_table, lam_init,
           norm1_g, w_in, q_norm_g, k_norm_g, lam_q1, lam_k1, lam_q2, lam_k2, subln_g, b_i, b_f,
           mnorm_g, w_a, w_b, w_o, norm2_g, peer_wq, subkeys, peer_u, peer_v):
    bp, t, d = xp.shape
    bs, ts, _ = xs.shape
    assert ts == 1
    n_p = bp * t
    nh, dk, dv = MLSTM_HEADS, MLSTM_DK, MLSTM_DV

    w_main, w_gate, qg_t, *prep_rest = _prep_in_proj(w_in, q_norm_g, k_norm_g, b_i, b_f)
    prep_p = (w_main, w_gate, qg_t * (DIFF_DH ** -0.5 * LOG2E), *prep_rest)
    prep_s = (w_main, w_gate, qg_t * (DIFF_DH ** -0.5), *prep_rest)
    g1 = norm1_g.reshape(1, d).astype(F32)
    lam_vecs = jnp.stack([lam_q1, lam_k1, lam_q2, lam_k2]).astype(F32)
    sg = subln_g.reshape(1, -1).astype(F32)
    mg = mnorm_g.reshape(1, -1).astype(F32)
    slopes = jnp.exp2(-8.0 * jnp.arange(1, DIFF_HEADS + 1, dtype=F32) / DIFF_HEADS)
    wa_b, wb_b, wo_b = w_a.astype(BF16), w_b.astype(BF16), w_o.astype(BF16)
    g2 = norm2_g.reshape(1, d).astype(F32)
    wq_t = peer_wq.T.astype(BF16)
    u_b = peer_u.astype(BF16)
    v_t = peer_v.T.astype(BF16)

    def peer(x2, xn_t, pq_t, tn):
        lrow, rank2, e1, e2 = _peer_topk(pq_t, subkeys, tn=LANES)
        return _peer_experts(xn_t, u_b, v_t, lrow, rank2, e1, e2, x2, tn=tn, te=8 * PEER_NKEYS)

    x2d = xp.reshape(n_p, d)
    qn, kf, kb, vf, vb, mqk, mv, og, gates, gcol, grow = _in_proj(
        x2d, g1, *prep_p, tm=_row_tile(n_p, 512), act_dtype=BF16)
    a = _prompt_attention(qn, kb, vb, slopes, lam_vecs, sg, batch=bp, seq=t, tq=_row_tile(t, 256),
                          lam_init=lam_init)
    hb, c_p, n_p_, m_p = _prompt_mlstm(mqk, mv, og, gcol, grow, mg, batch=bp, seq=t, chunk=_row_tile(t, 256))
    x2, xn_t, pq_t = _merge(x2d, a, hb, gates, wa_b, wb_b, wo_b, g2, wq_t, tm=_row_tile(n_p, 256))
    y_p = peer(x2, xn_t, pq_t, _row_tile(n_p, 256)).reshape(bp, t, d)
    k_p = kf.reshape(bp, t, DIFF_HEADS, 2 * DIFF_DH)
    v_p = vf.reshape(bp, t, DIFF_HEADS, DIFF_DV)
    m_p = m_p[:, :, 0]

    xs2d = xs.reshape(bs, d)
    qn, kf, _, vf, _, mqk, mv, og, gates, gcol, _ = _in_proj(xs2d, g1, *prep_s, tm=bs, act_dtype=F32)
    a = _decode_attention(page_table, qn, kf, vf, cache_k, cache_v, layer, lam_vecs, sg, lam_init=lam_init)
    hb, c_s, n_s, m_s = _mlstm_step(
        mqk[:, :nh * dk].reshape(bs, nh, dk, 1), mqk[:, nh * dk:].reshape(bs, nh, dk, 1),
        mv.reshape(bs, nh, 1, dv), og.reshape(bs, 1, nh * dv),
        gcol[:, :nh].reshape(bs, nh, 1, 1), gcol[:, nh:2 * nh].reshape(bs, nh, 1, 1),
        st_c.astype(F32), st_n.astype(F32).reshape(bs, nh, dk, 1), st_m.astype(F32).reshape(bs, nh, 1, 1), mg)
    pad = -bs % LANES
    padr = lambda z: jnp.pad(z, ((0, pad), (0, 0)))
    x2, xn_t, pq_t = _merge(padr(xs2d), padr(a.reshape(bs, -1)).astype(BF16), padr(hb.reshape(bs, -1)).astype(BF16),
                            padr(gates).astype(BF16), wa_b, wb_b, wo_b, g2, wq_t, tm=LANES)
    y_s = peer(x2, xn_t, pq_t, LANES)[:bs].reshape(bs, 1, d)
    k_s = kf.reshape(bs, 1, DIFF_HEADS, 2 * DIFF_DH)
    v_s = vf.reshape(bs, 1, DIFF_HEADS, DIFF_DV)
    return (y_p, y_s, k_p, v_p, c_p, n_p_, m_p, k_s, v_s, c_s, n_s.reshape(bs, nh, dk), m_s.reshape(bs, nh))


def kernel(x_prompt, x_sample, cache_k, cache_v, state_C, state_n, state_m, page_table, norm1_g, w_in,
           q_norm_g, k_norm_g, lam_q1, lam_k1, lam_q2, lam_k2, diff_subln_g, b_i, b_f, mlstm_norm_g,
           w_branch_a, w_branch_b, w_out, norm2_g, peer_wq, peer_subkeys, peer_u, peer_v):
    depth = w_in.shape[0]
    xp, xs = x_prompt, x_sample
    per_layer = []
    for l in range(depth):
        lam_init = 0.8 - 0.6 * math.exp(-0.3 * l)
        outs = _layer(xp, xs, cache_k, cache_v, l, state_C[l], state_n[l], state_m[l], page_table, lam_init,
                      norm1_g[l], w_in[l], q_norm_g[l], k_norm_g[l], lam_q1[l], lam_k1[l], lam_q2[l], lam_k2[l],
                      diff_subln_g[l], b_i[l], b_f[l], mlstm_norm_g[l], w_branch_a[l], w_branch_b[l], w_out[l],
                      norm2_g[l], peer_wq[l], peer_subkeys[l], peer_u[l], peer_v[l])
        xp, xs = outs[0], outs[1]
        per_layer.append(outs[2:])
    if depth == 1:
        stacked = tuple(o[None] for o in per_layer[0])
    else:
        stacked = tuple(jnp.stack([p[i] for p in per_layer]) for i in range(10))
    return (xp, xs) + stacked
```
